```python
import math
import jax, jax.numpy as jnp
from jax import lax
import numpy as np

D_MODEL = 4096
BATCH = 32
SEQ = 256
DEPTH = 2
DEC_BATCH = 2
DEC_SEQ = 2048
PAST_LEN = 512

GRID_W = 64
N_MIXERS = 2
N_POOL_LAYERS = (DEPTH + 1) // 2
N_SSD_LAYERS = DEPTH // 2
E_POOL = 2 * D_MODEL
POOL_WINDOWS = (2, 4, 8, 16)
N_POOL_GROUPS = len(POOL_WINDOWS)
POOL_GROUP_W = E_POOL // N_POOL_GROUPS
D_INNER = 2 * D_MODEL
HEAD_DIM = 64
N_HEADS = D_INNER // HEAD_DIM
D_STATE = 128
N_GROUPS = 8
HEADS_PER_GROUP = N_HEADS // N_GROUPS
CONV_CH = D_INNER + 2 * N_GROUPS * D_STATE
CONV_W = 7
CHUNK = 128
SSD_IN = D_INNER + CONV_CH + 2 * N_HEADS
EPS = 1e-6

kernel_name = 'hybrid_pool_ssd_diffusion_step'


def _silu(t):
    return t * jax.nn.sigmoid(t)


def _rmsnorm(t, w):
    t32 = t.astype(jnp.float32)
    t32 = t32 * lax.rsqrt(jnp.mean(t32 * t32, axis=-1, keepdims=True) + EPS)
    return (t32 * w.astype(jnp.float32)).astype(t.dtype)


def _bounds(n, w):
    t = np.arange(n)
    lo = np.clip(t - w // 2, 0, n)
    hi = np.clip(t - w // 2 + w, 0, n)
    return lo, hi, (hi - lo).astype(np.float32)


def _pool1d(x, w):
    b, L, C = x.shape
    x32 = x.astype(jnp.float32)
    S = jnp.concatenate([jnp.zeros((b, 1, C), jnp.float32), jnp.cumsum(x32, axis=1)], axis=1)
    lo, hi, cnt = _bounds(L, w)
    return ((S[:, hi] - S[:, lo]) / cnt[None, :, None]).astype(x.dtype)


def _pool2d(x, w):
    b, L, C = x.shape
    rows = L // GRID_W
    xr = x.astype(jnp.float32).reshape(b, rows, GRID_W, C)
    S = jnp.cumsum(jnp.cumsum(xr, axis=1), axis=2)
    S = jnp.pad(S, ((0, 0), (1, 0), (1, 0), (0, 0)))
    rlo, rhi, rc = _bounds(rows, w)
    clo, chi, cc = _bounds(GRID_W, w)
    s_hi = S[:, rhi]
    s_lo = S[:, rlo]
    tot = s_hi[:, :, chi] - s_hi[:, :, clo] - s_lo[:, :, chi] + s_lo[:, :, clo]
    cnt = rc[:, None] * cc[None, :]
    return (tot / cnt[None, :, :, None]).reshape(b, L, C).astype(x.dtype)


def _pool_branch(u, in_w, grp_w, grp_b, scale, out_w, grid):
    b, L, _ = u.shape
    proj = u @ in_w
    xp, z = proj[..., :E_POOL], proj[..., E_POOL:]
    xg = xp.reshape(b, L, N_POOL_GROUPS, POOL_GROUP_W)
    pool = _pool2d if grid else _pool1d
    mixed = jnp.stack([pool(xg[:, :, g], w) - xg[:, :, g] for g, w in enumerate(POOL_WINDOWS)], axis=2)
    mixed = jnp.einsum('blgc,gcd->blgd', mixed, grp_w) + grp_b
    mixed = mixed.reshape(b, L, E_POOL) * scale
    return (mixed * _silu(z)) @ out_w


def _dwconv(u, w, bias):
    out = lax.conv_general_dilated(u, w[:, None, :], window_strides=(1,),
                                   padding=[(CONV_W // 2, CONV_W // 2)],
                                   dimension_numbers=('NWC', 'WIO', 'NWC'),
                                   feature_group_count=u.shape[-1])
    return out + bias


def _ssd_scan(x, dt, A, B, C, h0):
    b, L, H, P = x.shape
    nc = L // CHUNK
    f32 = jnp.float32
    G, R, N = N_GROUPS, HEADS_PER_GROUP, D_STATE
    xdt = (x.astype(f32) * dt[..., None]).reshape(b, nc, CHUNK, G, R, P)
    a_cum = jnp.cumsum((dt * A).reshape(b, nc, CHUNK, G, R), axis=2)
    Bc = B.astype(f32).reshape(b, nc, CHUNK, G, N)
    Cc = C.astype(f32).reshape(b, nc, CHUNK, G, N)
    at = jnp.moveaxis(a_cum, 2, -1)
    diff = at[..., :, None] - at[..., None, :]
    mask = np.tril(np.ones((CHUNK, CHUNK), dtype=bool))
    decay = jnp.exp(jnp.where(mask, diff, -jnp.inf))
    scores = jnp.einsum('bclgn,bcsgn->bcgls', Cc, Bc)
    y_diag = jnp.einsum('bcgrls,bcsgrp->bclgrp', scores[:, :, :, None] * decay, xdt)
    decay_s = jnp.exp(a_cum[:, :, -1:] - a_cum)
    states = jnp.einsum('bclgn,bclgrp->bcgrpn', Bc, xdt * decay_s[..., None])
    chunk_decay = jnp.exp(a_cum[:, :, -1])

    def step(h, inp):
        s, d = inp
        return h * d[..., None, None] + s, h

    h_init = h0.astype(f32).reshape(b, G, R, P, N)
    h_fin, h_in = lax.scan(step, h_init, (jnp.moveaxis(states, 1, 0), jnp.moveaxis(chunk_decay, 1, 0)))
    h_in = jnp.moveaxis(h_in, 0, 1)
    y_off = jnp.einsum('bclgn,bcgrpn->bclgrp', Cc, h_in) * jnp.exp(a_cum)[..., None]
    y = (y_diag + y_off).reshape(b, L, H, P).astype(x.dtype)
    return y, h_fin.reshape(b, H, P, N).astype(h0.dtype)


def _ssd_branch(u, in_w, conv_w, conv_b, dt_bias, A_log, d_skip, gnorm_w, out_w, h0):
    b, L, _ = u.shape
    proj = u @ in_w
    z = proj[..., :D_INNER]
    xbc = _silu(_dwconv(proj[..., D_INNER:D_INNER + CONV_CH], conv_w, conv_b))
    dt_raw = proj[..., D_INNER + CONV_CH:]
    xs = xbc[..., :D_INNER].reshape(b, L, N_HEADS, HEAD_DIM)
    Bm = xbc[..., D_INNER:D_INNER + N_GROUPS * D_STATE].reshape(b, L, N_GROUPS, D_STATE)
    Cm = xbc[..., D_INNER + N_GROUPS * D_STATE:].reshape(b, L, N_GROUPS, D_STATE)
    dt = jax.nn.softplus(dt_raw.astype(jnp.float32).reshape(b, L, 2, N_HEADS) + dt_bias.astype(jnp.float32))
    A = -jnp.exp(A_log.astype(jnp.float32))
    flip = lambda t: jnp.flip(t, axis=1)
    y_f, h_f = _ssd_scan(xs, dt[:, :, 0], A[0], Bm, Cm, h0[:, 0])
    y_b, h_b = _ssd_scan(flip(xs), flip(dt[:, :, 1]), A[1], flip(Bm), flip(Cm), h0[:, 1])
    y = y_f + flip(y_b) + xs * d_skip[:, None]
    y = _rmsnorm(y.reshape(b, L, D_INNER) * _silu(z), gnorm_w)
    return y @ out_w, jnp.stack([h_f, h_b], axis=1)


def _modulate(t, mod):
    shift, scale, gate = jnp.split(mod, 3, axis=-1)
    return t * (1 + scale) + shift, gate


def setup_inputs(seed: int = 0) -> dict:
    key = jax.random.key(seed)
    ks = jax.random.split(key, 22)
    nrm = jax.random.normal
    D = D_MODEL
    dt0 = jnp.exp(jax.random.uniform(ks[16], (N_SSD_LAYERS, 2, N_HEADS),
                                     minval=math.log(1e-3), maxval=math.log(1e-1)))
    return {
        'x_prompt': nrm(ks[0], (BATCH, SEQ, D)),
        'x_sample': nrm(ks[1], (DEC_BATCH, DEC_SEQ, D)),
        'state_ssd': 0.3 * nrm(ks[2], (DEC_BATCH, N_SSD_LAYERS, 2, N_HEADS, HEAD_DIM, D_STATE)),
        'c': nrm(ks[3], (DEC_BATCH, D)),
        'c_ctx': nrm(ks[4], (D,)),
        'ada_w': nrm(ks[5], (DEPTH, D, 3 * D)) * (0.5 * D ** -0.5),
        'ada_b': 0.02 * nrm(ks[6], (DEPTH, 3 * D)),
        'norm_w': 1.0 + 0.02 * nrm(ks[7], (DEPTH, D)),
        'pool_in_w': nrm(ks[8], (N_POOL_LAYERS, D, 2 * E_POOL)) * D ** -0.5,
        'pool_grp_w': nrm(ks[9], (N_POOL_LAYERS, N_POOL_GROUPS, POOL_GROUP_W, POOL_GROUP_W)) * POOL_GROUP_W ** -0.5,
        'pool_grp_b': 0.02 * nrm(ks[10], (N_POOL_LAYERS, N_POOL_GROUPS, POOL_GROUP_W)),
        'pool_scale': 1.0 + 0.02 * nrm(ks[11], (N_POOL_LAYERS, E_POOL)),
        'pool_out_w': nrm(ks[12], (N_POOL_LAYERS, E_POOL, D)) * E_POOL ** -0.5,
        'ssd_in_w': nrm(ks[13], (N_SSD_LAYERS, D, SSD_IN)) * D ** -0.5,
        'ssd_conv_w': nrm(ks[14], (N_SSD_LAYERS, CONV_W, CONV_CH)) * CONV_W ** -0.5,
        'ssd_conv_b': 0.02 * nrm(ks[15], (N_SSD_LAYERS, CONV_CH)),
        'ssd_dt_bias': dt0 + jnp.log(-jnp.expm1(-dt0)),
        'ssd_A_log': jnp.log(jax.random.uniform(ks[17], (N_SSD_LAYERS, 2, N_HEADS), minval=1.0, maxval=16.0)),
        'ssd_D': 1.0 + 0.1 * nrm(ks[18], (N_SSD_LAYERS, N_HEADS)),
        'ssd_norm_w': 1.0 + 0.02 * nrm(ks[19], (N_SSD_LAYERS, D_INNER)),
        'ssd_out_w': nrm(ks[20], (N_SSD_LAYERS, D_INNER, D)) * D_INNER ** -0.5,
        'final_norm_w': 1.0 + 0.02 * nrm(ks[21], (D,)),
    }


def reference(x_prompt, x_sample, state_ssd, c, c_ctx, ada_w, ada_b, norm_w,
              pool_in_w, pool_grp_w, pool_grp_b, pool_scale, pool_out_w,
              ssd_in_w, ssd_conv_w, ssd_conv_b, ssd_dt_bias, ssd_A_log, ssd_D,
              ssd_norm_w, ssd_out_w, final_norm_w):
    h_ctx, h_lat = x_prompt, x_sample
    new_states = []
    for i in range(DEPTH):
        j = i // N_MIXERS
        mod_ctx = _silu(c_ctx) @ ada_w[i] + ada_b[i]
        mod_lat = (_silu(c) @ ada_w[i] + ada_b[i])[:, None, :]
        u_ctx, g_ctx = _modulate(_rmsnorm(h_ctx, norm_w[i]), mod_ctx)
        u_lat, g_lat = _modulate(_rmsnorm(h_lat, norm_w[i]), mod_lat)
        if i % N_MIXERS == 0:
            pw = (pool_in_w[j], pool_grp_w[j], pool_grp_b[j], pool_scale[j], pool_out_w[j])
            o_ctx = _pool_branch(u_ctx, *pw, grid=False)
            o_lat = _pool_branch(u_lat, *pw, grid=True)
        else:
            sw = (ssd_in_w[j], ssd_conv_w[j], ssd_conv_b[j], ssd_dt_bias[j], ssd_A_log[j],
                  ssd_D[j], ssd_norm_w[j], ssd_out_w[j])
            h0_ctx = jnp.zeros((u_ctx.shape[0], 2, N_HEADS, HEAD_DIM, D_STATE), state_ssd.dtype)
            o_ctx, st_ctx = _ssd_branch(u_ctx, *sw, h0_ctx)
            o_lat, _ = _ssd_branch(u_lat, *sw, state_ssd[:, j])
            new_states.append(st_ctx)
        h_ctx = h_ctx + g_ctx * o_ctx
        h_lat = h_lat + g_lat * o_lat
    y_prompt = _rmsnorm(h_ctx, final_norm_w)
    y_sample = _rmsnorm(h_lat, final_norm_w)
    new_state_ssd = jnp.stack(new_states, axis=1)
    return (y_prompt, y_sample, new_state_ssd)
```

```python
import functools

import jax
import jax.numpy as jnp
from jax import lax
from jax.experimental import pallas as pl
from jax.experimental.pallas import tpu as pltpu

F32 = jnp.float32
BF16 = jnp.bfloat16

EPS = 1e-6
POOL_WINDOWS = (2, 4, 8, 16)
GRID_W = 64
HEAD_DIM = 64
D_STATE = 128
N_GROUPS = 8
CONV_W = 7
CHUNK = 128
LANES = 128

V7X_VMEM_BYTES = 64 * 2**20
_VMEM_HEADROOM = 6 * 2**20


def _cparams(semantics, vmem_need):
    limit = min(max(int(vmem_need), 16 * 2**20), V7X_VMEM_BYTES - _VMEM_HEADROOM)
    return pltpu.CompilerParams(dimension_semantics=semantics, vmem_limit_bytes=limit)


def _silu(t):
    return t * jax.nn.sigmoid(t)


def _seg(tile, rows_per_tile, n_ctx, dec_seq):
    start = tile * rows_per_tile
    return jnp.where(start < n_ctx, 0, 1 + jnp.maximum(start - n_ctx, 0) // dec_seq)


def _ada_kernel(c_ref, w_ref, b_ref, o_ref):
    s = _silu(c_ref[...]).astype(BF16)
    o_ref[0] = jnp.dot(s, w_ref[0].astype(BF16), preferred_element_type=F32) + b_ref[0]


def _ada_mod(c_rows, ada_w, ada_b, bn=512):
    depth, d, n3 = ada_w.shape
    rows = c_rows.shape[0]
    need = 2 * (d * bn * 4) + d * bn * 2 + 4 * rows * d * 4
    return pl.pallas_call(
        _ada_kernel,
        grid=(depth, n3 // bn),
        in_specs=[
            pl.BlockSpec((rows, d), lambda l, j: (0, 0)),
            pl.BlockSpec((1, d, bn), lambda l, j: (l, 0, j)),
            pl.BlockSpec((1, 1, bn), lambda l, j: (l, 0, j)),
        ],
        out_specs=pl.BlockSpec((1, rows, bn), lambda l, j: (l, 0, j)),
        out_shape=jax.ShapeDtypeStruct((depth, rows, n3), F32),
        compiler_params=_cparams(("arbitrary", "arbitrary"), need + 8 * 2**20),
        name="ada_mod",
    )(c_rows, ada_w, ada_b.reshape(depth, 1, n3))


def _norm_mod(x, w, shift, scale):
    xn = x * lax.rsqrt(jnp.mean(x * x, axis=-1, keepdims=True) + EPS) * w
    return xn * (1.0 + scale) + shift


def _prologue_kernel(xc_ref, xl_ref, w_ref, sh_ref, sc_ref, u_ref, *, ctx_tiles):
    x = jnp.where(pl.program_id(0) < ctx_tiles, xc_ref[...], xl_ref[...])
    u_ref[...] = _norm_mod(x, w_ref[...], sh_ref[0], sc_ref[0]).astype(BF16)


def _prologue(h_ctx, h_lat, w, shift, scale, dec_seq, tm=256):
    n_ctx, d = h_ctx.shape
    t = n_ctx + h_lat.shape[0]
    ctx_tiles = n_ctx // tm
    seg = lambda i: (_seg(i, tm, n_ctx, dec_seq), 0, 0)
    need = 2 * (2 * tm * d * 4 + tm * d * 2) + 6 * tm * d * 4
    return pl.pallas_call(
        functools.partial(_prologue_kernel, ctx_tiles=ctx_tiles),
        grid=(t // tm,),
        in_specs=[
            pl.BlockSpec((tm, d), lambda i: (jnp.minimum(i, ctx_tiles - 1), 0)),
            pl.BlockSpec((tm, d), lambda i: (jnp.maximum(i - ctx_tiles, 0), 0)),
            pl.BlockSpec((1, d), lambda i: (0, 0)),
            pl.BlockSpec((1, 1, d), seg),
            pl.BlockSpec((1, 1, d), seg),
        ],
        out_specs=pl.BlockSpec((tm, d), lambda i: (i, 0)),
        out_shape=jax.ShapeDtypeStruct((t, d), BF16),
        compiler_params=_cparams(("arbitrary",), need),
        name="prologue",
    )(h_ctx, h_lat, w, shift, scale)


def _mm_kernel(a_ref, b_ref, o_ref):
    o_ref[...] = jnp.dot(a_ref[...], b_ref[...], preferred_element_type=F32).astype(o_ref.dtype)


def _matmul(a, b, out_dtype, bm, bn, a_kblock=0):
    m = a.shape[0]
    k, n = b.shape
    bm, bn = min(bm, m), min(bn, n)
    osz = jnp.dtype(out_dtype).itemsize
    need = 2 * (bm * k * 2 + k * bn * 2 + bm * bn * osz) + bm * bn * 4
    return pl.pallas_call(
        _mm_kernel,
        grid=(m // bm, n // bn),
        in_specs=[
            pl.BlockSpec((bm, k), lambda i, j: (i, a_kblock)),
            pl.BlockSpec((k, bn), lambda i, j: (0, j)),
        ],
        out_specs=pl.BlockSpec((bm, bn), lambda i, j: (i, j)),
        out_shape=jax.ShapeDtypeStruct((m, n), out_dtype),
        compiler_params=_cparams(("arbitrary", "arbitrary"), need + 4 * 2**20),
        name="matmul",
    )(a, b)


def _mm_group_kernel(a_ref, b_ref, o_ref):
    o_ref[...] = jnp.dot(a_ref[...], b_ref[0], preferred_element_type=F32).astype(o_ref.dtype)


def _matmul_grouped(a, b, out_dtype, bm, bn):
    m = a.shape[0]
    groups, k, n = b.shape
    bm, bn = min(bm, m), min(bn, n)
    nj = n // bn
    osz = jnp.dtype(out_dtype).itemsize
    need = 2 * (bm * k * 2 + k * bn * 2 + bm * bn * osz) + bm * bn * 4
    return pl.pallas_call(
        _mm_group_kernel,
        grid=(groups, m // bm, nj),
        in_specs=[
            pl.BlockSpec((bm, k), lambda g, i, j: (i, g)),
            pl.BlockSpec((1, k, bn), lambda g, i, j: (g, 0, j)),
        ],
        out_specs=pl.BlockSpec((bm, bn), lambda g, i, j: (i, g * nj + j)),
        out_shape=jax.ShapeDtypeStruct((m, groups * n), out_dtype),
        compiler_params=_cparams(("arbitrary", "arbitrary", "arbitrary"), need + 4 * 2**20),
        name="matmul_grouped",
    )(a, b)


def _mm_res_kernel(a_ref, b_ref, h_ref, g_ref, o_ref, acc_ref, *, nk):
    k = pl.program_id(2)

    @pl.when(k == 0)
    def _():
        acc_ref[...] = jnp.zeros_like(acc_ref)

    acc_ref[...] += jnp.dot(a_ref[...], b_ref[...], preferred_element_type=F32)

    @pl.when(k == nk - 1)
    def _():
        o_ref[...] = h_ref[...] + g_ref[0] * acc_ref[...]


def _matmul_residual(a, b, h, gate, seg0, seg_rows, bm, bn, bk):
    m, kdim = a.shape
    n = b.shape[1]
    bm, bn, bk = min(bm, m), min(bn, n), min(bk, kdim)
    nk = kdim // bk
    need = 2 * (bm * bk * 2 + bk * bn * 2 + 2 * bm * bn * 4) + 2 * bm * bn * 4
    return pl.pallas_call(
        functools.partial(_mm_res_kernel, nk=nk),
        grid=(m // bm, n // bn, nk),
        in_specs=[
            pl.BlockSpec((bm, bk), lambda i, j, k: (i, k)),
            pl.BlockSpec((bk, bn), lambda i, j, k: (k, j)),
            pl.BlockSpec((bm, bn), lambda i, j, k: (i, j)),
            pl.BlockSpec((1, 1, bn), lambda i, j, k: (seg0 + (i * bm) // seg_rows, 0, j)),
        ],
        out_specs=pl.BlockSpec((bm, bn), lambda i, j, k: (i, j)),
        out_shape=jax.ShapeDtypeStruct((m, n), F32),
        scratch_shapes=[pltpu.VMEM((bm, bn), F32)],
        compiler_params=_cparams(("arbitrary", "arbitrary", "arbitrary"), need + 4 * 2**20),
        name="matmul_residual",
    )(a, b, h, gate)


def _window_members(pos_t, pos_s, w):
    lo = pos_t - w // 2
    return (pos_s >= lo) & (pos_s < lo + w)


def _pool_membership(seq, grid_w):
    t = jnp.arange(seq, dtype=jnp.int32)[:, None]
    s = jnp.arange(seq, dtype=jnp.int32)[None, :]
    mats = []
    for w in POOL_WINDOWS:
        if grid_w is None:
            m = _window_members(t, s, w)
        else:
            m = _window_members(t // grid_w, s // grid_w, w) & _window_members(t % grid_w, s % grid_w, w)
        mats.append(m)
    return jnp.stack(mats).astype(BF16)


def _poolmix_kernel(y_ref, z_ref, p_ref, b_ref, sc_ref, o_ref, *, seq, nseq, rows_step):
    bias = b_ref[...]
    scale = sc_ref[...]
    for s in range(nseq):
        y_seq = y_ref[pl.ds(s * seq, seq), :]
        for r0 in range(0, seq, rows_step):
            member = p_ref[0, pl.ds(r0, rows_step), :]
            cnt = jnp.sum(member.astype(F32), axis=1, keepdims=True)
            rows = pl.ds(s * seq + r0, rows_step)
            tot = jnp.dot(member, y_seq, preferred_element_type=F32)
            mixed = tot / cnt - y_ref[rows, :].astype(F32) + bias
            z = z_ref[rows, :].astype(F32)
            o_ref[rows, :] = (mixed * scale * _silu(z)).astype(BF16)


def _poolmix(y, proj, z_col0, member, grp_b, scale, row0, n_rows, seq, nseq, cb):
    e = y.shape[1]
    gw = e // len(POOL_WINDOWS)
    cb = min(cb, gw)
    rows = seq * nseq
    r0 = row0 // rows
    zc0 = z_col0 // cb
    rows_step = min(seq, 512)
    need = 2 * (3 * rows * cb * 2 + seq * seq * 2) + 6 * rows_step * cb * 4 + rows_step * seq * 4
    return pl.pallas_call(
        functools.partial(_poolmix_kernel, seq=seq, nseq=nseq, rows_step=rows_step),
        grid=(e // cb, n_rows // rows),
        in_specs=[
            pl.BlockSpec((rows, cb), lambda c, i: (i + r0, c)),
            pl.BlockSpec((rows, cb), lambda c, i: (i + r0, c + zc0)),
            pl.BlockSpec((1, seq, seq), lambda c, i: (c * cb // gw, 0, 0)),
            pl.BlockSpec((1, cb), lambda c, i: (0, c)),
            pl.BlockSpec((1, cb), lambda c, i: (0, c)),
        ],
        out_specs=pl.BlockSpec((rows, cb), lambda c, i: (i, c)),
        out_shape=jax.ShapeDtypeStruct((n_rows, e), BF16),
        compiler_params=_cparams(("arbitrary", "arbitrary"), need + 8 * 2**20),
        name="poolmix",
    )(y, proj, member, grp_b.reshape(1, e), scale.reshape(1, e))


def _conv_kernel(x_ref, w_ref, b_ref, o_ref, *, seq):
    x = x_ref[...].astype(F32)
    rows, cb = x.shape
    pos = lax.broadcasted_iota(jnp.int32, (rows, cb), 0) & (seq - 1)
    half = CONV_W // 2
    acc = b_ref[...] + w_ref[pl.ds(half, 1), :] * x
    for k in range(CONV_W):
        d = k - half
        if d == 0:
            continue
        shifted = pltpu.roll(x, (-d) % rows, axis=0)
        valid = (pos + d >= 0) & (pos + d < seq)
        acc = acc + w_ref[pl.ds(k, 1), :] * jnp.where(valid, shifted, 0.0)
    o_ref[...] = _silu(acc).astype(BF16)


def _conv_silu(proj, col0, n_cols, conv_w, conv_b, row0, n_rows, seq, rows, cb=512):
    assert seq & (seq - 1) == 0 and rows % seq == 0
    r0 = row0 // rows
    c0 = col0 // cb
    need = 2 * (2 * rows * cb * 2) + 8 * rows * cb * 4
    return pl.pallas_call(
        functools.partial(_conv_kernel, seq=seq),
        grid=(n_rows // rows, n_cols // cb),
        in_specs=[
            pl.BlockSpec((rows, cb), lambda i, c: (i + r0, c + c0)),
            pl.BlockSpec((CONV_W, cb), lambda i, c: (0, c)),
            pl.BlockSpec((1, cb), lambda i, c: (0, c)),
        ],
        out_specs=pl.BlockSpec((rows, cb), lambda i, c: (i, c)),
        out_shape=jax.ShapeDtypeStruct((n_rows, n_cols), BF16),
        compiler_params=_cparams(("arbitrary", "arbitrary"), need + 4 * 2**20),
        name="conv_silu",
    )(proj, conv_w, conv_b.reshape(1, n_cols))


def _split3(v):
    hi = v.astype(BF16)
    r1 = v - hi.astype(F32)
    mid = r1.astype(BF16)
    lo = (r1 - mid.astype(F32)).astype(BF16)
    return hi, mid, lo


def _dot3_rhs(lhs, v):
    return sum(jnp.dot(lhs, p, preferred_element_type=F32) for p in _split3(v))


def _dot3_lhs(v, rhs):
    return sum(jnp.dot(p, rhs, preferred_element_type=F32) for p in _split3(v))


def _ssd_kernel(*refs, nc, hpg, has_h0, want_state):
    it = iter(refs)
    x_ref, b_ref, c_ref, dt_ref, dtb_ref, alog_ref, dskip_ref = (next(it) for _ in range(7))
    h0_ref = next(it) if has_h0 else None
    y_ref = next(it)
    st_ref = next(it) if want_state else None
    h_scr = next(it)

    q = CHUNK
    assert q == D_STATE == LANES and hpg % 2 == 0 and 2 * HEAD_DIM == LANES
    width = hpg * HEAD_DIM
    g = pl.program_id(1)
    lane_shift = (LANES - hpg * g) % LANES

    row_i = lax.broadcasted_iota(jnp.int32, (q, q), 0)
    col_i = lax.broadcasted_iota(jnp.int32, (q, q), 1)
    masks = (row_i >= col_i, row_i <= col_i)
    tri = tuple(m.astype(BF16) for m in masks)
    edge_row = (q - 1, 0)
    e_row = lax.broadcasted_iota(jnp.int32, (LANES, width), 0)
    e_col = lax.broadcasted_iota(jnp.int32, (LANES, width), 1) // HEAD_DIM
    expand = (e_row == e_col).astype(BF16)
    head_lane = lax.broadcasted_iota(jnp.int32, (q, LANES), 1) < hpg
    head_lane8 = lax.broadcasted_iota(jnp.int32, (8, LANES), 1) < hpg
    lane_half = lax.broadcasted_iota(jnp.int32, (q + D_STATE, LANES), 1) < HEAD_DIM

    def group_lanes(v):
        return pltpu.roll(v, lane_shift, axis=1)

    neg_a = [-jnp.exp(group_lanes(alog_ref[pl.ds(d, 1), :])) for d in range(2)]
    dt_bias = [group_lanes(dtb_ref[pl.ds(d, 1), :]) for d in range(2)]

    y_ref[...] = x_ref[...].astype(F32) * dskip_ref[...]
    for d in range(2):
        if has_h0:
            h_scr[d] = jnp.transpose(h0_ref[0, d])
        else:
            h_scr[d] = jnp.zeros((D_STATE, width), F32)

    def chunk_step(i, carry):
        for d in range(2):
            c = i if d == 0 else nc - 1 - i
            rows = pl.ds(pl.multiple_of(c * q, q), q)
            xc = x_ref[rows, :]
            bc = b_ref[rows, :]
            cc = c_ref[rows, :]
            dt_raw = group_lanes(dt_ref[rows, pl.ds(d * LANES, LANES)])
            dt = jax.nn.softplus(dt_raw + dt_bias[d])
            a = jnp.where(head_lane, dt * neg_a[d], 0.0)
            cum = _dot3_rhs(tri[d], a)
            cum_t = jnp.transpose(cum)
            dt_t = jnp.transpose(dt)
            edge = cum[edge_row[d]:edge_row[d] + 1, :]
            scores = lax.dot_general(cc, bc, (((1,), (1,)), ((), ())), preferred_element_type=F32)
            h_in = h_scr[d]
            h_bf = h_in.astype(BF16)
            cc_f = cc.astype(F32)
            for p in range(hpg // 2):
                lanes = pl.ds(p * LANES, LANES)
                lhs_parts = []
                for r in (2 * p, 2 * p + 1):
                    col = jnp.broadcast_to(cum[:, r:r + 1], (q, q))
                    decay = jnp.where(masks[d], jnp.exp(col - cum_t[r:r + 1, :]), 0.0)
                    m = scores * decay * dt_t[r:r + 1, :]
                    lhs_parts += [m.astype(BF16), (cc_f * jnp.exp(col)).astype(BF16)]
                x_pair = jnp.concatenate([x_ref[rows, lanes], h_bf[:, p * LANES:(p + 1) * LANES]], axis=0)
                zero = jnp.zeros_like(x_pair)
                rhs = jnp.concatenate([jnp.where(lane_half, x_pair, zero), jnp.where(lane_half, zero, x_pair)], axis=0)
                y_ref[rows, lanes] += jnp.dot(jnp.concatenate(lhs_parts, axis=1), rhs, preferred_element_type=F32)
            w_state = jnp.where(head_lane, dt * jnp.exp(edge - cum), 0.0)
            xd = (xc.astype(F32) * _dot3_lhs(w_state, expand)).astype(BF16)
            bc_t = jnp.transpose(bc.astype(F32)).astype(BF16)
            states = jnp.dot(bc_t, xd, preferred_element_type=F32)
            chunk_decay = jnp.where(head_lane8, jnp.broadcast_to(jnp.exp(edge), (8, LANES)), 0.0)
            h_scr[d] = h_in * _dot3_lhs(chunk_decay, expand)[:1, :] + states
        return carry

    lax.fori_loop(0, nc, chunk_step, 0)

    if want_state:
        for d in range(2):
            st_ref[0, d] = jnp.transpose(h_scr[d])


def _ssd(xbc, dt_raw, dt_bias, a_log, d_skip_exp, h0, dt_row0, n_seq, seq, n_heads, want_state):
    d_inner = n_heads * HEAD_DIM
    hpg = n_heads // N_GROUPS
    width = hpg * HEAD_DIM
    nc = seq // CHUNK
    r0 = dt_row0 // seq
    b0 = d_inner // D_STATE
    has_h0 = h0 is not None
    in_specs = [
        pl.BlockSpec((seq, width), lambda s, g: (s, g)),
        pl.BlockSpec((seq, D_STATE), lambda s, g: (s, b0 + g)),
        pl.BlockSpec((seq, D_STATE), lambda s, g: (s, b0 + N_GROUPS + g)),
        pl.BlockSpec((seq, 2 * n_heads), lambda s, g: (s + r0, 0)),
        pl.BlockSpec((2, n_heads), lambda s, g: (0, 0)),
        pl.BlockSpec((2, n_heads), lambda s, g: (0, 0)),
        pl.BlockSpec((1, width), lambda s, g: (0, g)),
    ]
    args = [xbc, xbc, xbc, dt_raw, dt_bias, a_log, d_skip_exp]
    if has_h0:
        in_specs.append(pl.BlockSpec((1, 2, width, D_STATE), lambda s, g: (s, 0, g, 0)))
        args.append(h0)
    out_specs = [pl.BlockSpec((seq, width), lambda s, g: (s, g))]
    out_shape = [jax.ShapeDtypeStruct((n_seq * seq, d_inner), F32)]
    if want_state:
        out_specs.append(pl.BlockSpec((1, 2, width, D_STATE), lambda s, g: (s, 0, g, 0)))
        out_shape.append(jax.ShapeDtypeStruct((n_seq, 2, d_inner, D_STATE), F32))
    need = (2 * (seq * width * 2 + 2 * seq * D_STATE * 2 + seq * 2 * n_heads * 4 + seq * width * 4)
            + 4 * 2 * width * D_STATE * 4 + 2 * D_STATE * width * 4 + 16 * CHUNK * width * 4)
    return pl.pallas_call(
        functools.partial(_ssd_kernel, nc=nc, hpg=hpg, has_h0=has_h0, want_state=want_state),
        grid=(n_seq, N_GROUPS),
        in_specs=in_specs,
        out_specs=out_specs,
        out_shape=out_shape,
        scratch_shapes=[pltpu.VMEM((2, D_STATE, width), F32)],
        compiler_params=_cparams(("arbitrary", "arbitrary"), need + 8 * 2**20),
        name="ssd_scan",
    )(*args)


def _gnorm_kernel(y_ref, z_ref, w_ref, o_ref):
    t = y_ref[...] * _silu(z_ref[...].astype(F32))
    o_ref[...] = (t * lax.rsqrt(jnp.mean(t * t, axis=-1, keepdims=True) + EPS) * w_ref[...]).astype(BF16)


def _gated_norm(y, proj, w, row0, tm=128):
    rows, e = y.shape
    r0 = row0 // tm
    need = 2 * (tm * e * 4 + 2 * tm * e * 2) + 6 * tm * e * 4
    return pl.pallas_call(
        _gnorm_kernel,
        grid=(rows // tm,),
        in_specs=[
            pl.BlockSpec((tm, e), lambda i: (i, 0)),
            pl.BlockSpec((tm, e), lambda i: (i + r0, 0)),
            pl.BlockSpec((1, e), lambda i: (0, 0)),
        ],
        out_specs=pl.BlockSpec((tm, e), lambda i: (i, 0)),
        out_shape=jax.ShapeDtypeStruct((rows, e), BF16),
        compiler_params=_cparams(("arbitrary",), need),
        name="gated_norm",
    )(y, proj, w.reshape(1, e))


def _final_norm_kernel(x_ref, w_ref, o_ref):
    x = x_ref[...]
    o_ref[...] = x * lax.rsqrt(jnp.mean(x * x, axis=-1, keepdims=True) + EPS) * w_ref[...]


def _final_norm(h, w, tm=256):
    rows, d = h.shape
    need = 2 * (2 * tm * d * 4) + 4 * tm * d * 4
    return pl.pallas_call(
        _final_norm_kernel,
        grid=(rows // tm,),
        in_specs=[pl.BlockSpec((tm, d), lambda i: (i, 0)), pl.BlockSpec((1, d), lambda i: (0, 0))],
        out_specs=pl.BlockSpec((tm, d), lambda i: (i, 0)),
        out_shape=jax.ShapeDtypeStruct((rows, d), F32),
        compiler_params=_cparams(("arbitrary",), need),
        name="final_norm",
    )(h, w.reshape(1, d))


def kernel(x_prompt, x_sample, state_ssd, c, c_ctx, ada_w, ada_b, norm_w, pool_in_w, pool_grp_w, pool_grp_b, pool_scale, pool_out_w, ssd_in_w, ssd_conv_w, ssd_conv_b, ssd_dt_bias, ssd_A_log, ssd_D, ssd_norm_w, ssd_out_w, final_norm_w):
    batch, seq, d = x_prompt.shape
    dec_batch, dec_seq, _ = x_sample.shape
    n_ctx, n_lat = batch * seq, dec_batch * dec_seq
    depth = ada_w.shape[0]
    n_heads = ssd_A_log.shape[-1]
    d_inner = n_heads * HEAD_DIM
    e_pool = pool_out_w.shape[1]

    c_rows = jnp.concatenate([c_ctx[None, :], c, jnp.zeros((8 - 1 - dec_batch, d), F32)], axis=0)
    mod = _ada_mod(c_rows, ada_w, ada_b).reshape(depth, 8, 3, d)
    nseg = 1 + dec_batch
    part = lambda layer, which: mod[layer, :nseg, which][:, None, :]

    h_ctx = x_prompt.reshape(n_ctx, d)
    h_lat = x_sample.reshape(n_lat, d)
    new_states = []
    for layer in range(depth):
        j = layer // 2
        shift, scale, gate = (part(layer, k) for k in range(3))
        u = _prologue(h_ctx, h_lat, norm_w[layer].reshape(1, d), shift, scale, dec_seq)

        if layer % 2 == 0:
            proj = _matmul(u, pool_in_w[j].astype(BF16), BF16, bm=1024, bn=1024)
            y = _matmul_grouped(proj, pool_grp_w[j].astype(BF16), BF16, bm=1024, bn=1024)
            v_ctx = _poolmix(y, proj, e_pool, _pool_membership(seq, None), pool_grp_b[j], pool_scale[j],
                             row0=0, n_rows=n_ctx, seq=seq, nseq=4, cb=1024)
            v_lat = _poolmix(y, proj, e_pool, _pool_membership(dec_seq, GRID_W), pool_grp_b[j], pool_scale[j],
                             row0=n_ctx, n_rows=n_lat, seq=dec_seq, nseq=1, cb=512)
            w_out = pool_out_w[j].astype(BF16)
        else:
            w_in = ssd_in_w[j]
            n_main = 2 * d_inner + 2 * N_GROUPS * D_STATE
            proj = _matmul(u, w_in[:, :n_main].astype(BF16), BF16, bm=1024, bn=1024)
            dt_raw = _matmul(u, w_in[:, n_main:].astype(BF16), F32, bm=1024, bn=2 * n_heads)
            n_conv = n_main - d_inner
            d_exp = jnp.repeat(ssd_D[j], HEAD_DIM).reshape(1, d_inner)
            xbc_ctx = _conv_silu(proj, d_inner, n_conv, ssd_conv_w[j], ssd_conv_b[j], 0, n_ctx, seq, rows=4 * seq)
            xbc_lat = _conv_silu(proj, d_inner, n_conv, ssd_conv_w[j], ssd_conv_b[j], n_ctx, n_lat, dec_seq, rows=dec_seq)
            y_ctx, st = _ssd(xbc_ctx, dt_raw, ssd_dt_bias[j], ssd_A_log[j], d_exp, None, 0, batch, seq, n_heads, True)
            h0 = state_ssd[:, j].reshape(dec_batch, 2, d_inner, D_STATE)
            (y_lat,) = _ssd(xbc_lat, dt_raw, ssd_dt_bias[j], ssd_A_log[j], d_exp, h0, n_ctx, dec_batch, dec_seq, n_heads, False)
            new_states.append(st.reshape(batch, 2, n_heads, HEAD_DIM, D_STATE))
            v_ctx = _gated_norm(y_ctx, proj, ssd_norm_w[j], 0)
            v_lat = _gated_norm(y_lat, proj, ssd_norm_w[j], n_ctx)
            w_out = ssd_out_w[j].astype(BF16)

        h_ctx = _matmul_residual(v_ctx, w_out, h_ctx, gate, 0, n_ctx, bm=1024, bn=1024, bk=2048)
        h_lat = _matmul_residual(v_lat, w_out, h_lat, gate, 1, dec_seq, bm=1024, bn=1024, bk=2048)

    y_prompt = _final_norm(h_ctx, final_norm_w).reshape(batch, seq, d)
    y_sample = _final_norm(h_lat, final_norm_w).reshape(dec_batch, dec_seq, d)
    return (y_prompt, y_sample, jnp.stack(new_states, axis=1))
```

```python
import functools

import jax
import jax.numpy as jnp
from jax import lax
from jax.experimental import pallas as pl
from jax.experimental.pallas import tpu as pltpu

F32 = jnp.float32
BF16 = jnp.bfloat16

EPS = 1e-6
POOL_WINDOWS = (2, 4, 8, 16)
GRID_W = 64
HEAD_DIM = 64
D_STATE = 128
N_GROUPS = 8
CONV_W = 7
CHUNK = 128
LANES = 128

V7X_VMEM_BYTES = 64 * 2**20
_VMEM_HEADROOM = 6 * 2**20

_TILES = dict(
    ada_bn=512,
    norm_rows=256,
    proj=(1024, 1024),
    out=(1024, 1024, 2048),
    pool_ctx=(4, 1024),
    pool_lat=(1, 512),
    conv_cols=512,
)


def _cparams(semantics, vmem_need):
    limit = min(max(int(vmem_need), 16 * 2**20), V7X_VMEM_BYTES - _VMEM_HEADROOM)
    return pltpu.CompilerParams(dimension_semantics=semantics, vmem_limit_bytes=limit)


def _silu(t):
    return t * jax.nn.sigmoid(t)


def _seg(tile, rows_per_tile, n_ctx, dec_seq):
    start = tile * rows_per_tile
    return jnp.where(start < n_ctx, 0, 1 + jnp.maximum(start - n_ctx, 0) // dec_seq)


def _ada_kernel(c_ref, w_ref, b_ref, o_ref):
    s = _silu(c_ref[...]).astype(BF16)
    o_ref[0] = jnp.dot(s, w_ref[0].astype(BF16), preferred_element_type=F32) + b_ref[0]


def _ada_mod(c_rows, ada_w, ada_b):
    depth, d, n3 = ada_w.shape
    rows = c_rows.shape[0]
    bn = min(_TILES["ada_bn"], n3)
    need = 2 * (d * bn * 4) + d * bn * 2 + 4 * rows * d * 4
    return pl.pallas_call(
        _ada_kernel,
        grid=(depth, n3 // bn),
        in_specs=[
            pl.BlockSpec((rows, d), lambda l, j: (0, 0)),
            pl.BlockSpec((1, d, bn), lambda l, j: (l, 0, j)),
            pl.BlockSpec((1, 1, bn), lambda l, j: (l, 0, j)),
        ],
        out_specs=pl.BlockSpec((1, rows, bn), lambda l, j: (l, 0, j)),
        out_shape=jax.ShapeDtypeStruct((depth, rows, n3), F32),
        compiler_params=_cparams(("arbitrary", "arbitrary"), need + 8 * 2**20),
        name="ada_mod",
    )(c_rows, ada_w, ada_b.reshape(depth, 1, n3))


def _norm_mod(x, w, shift, scale):
    xn = x * lax.rsqrt(jnp.mean(x * x, axis=-1, keepdims=True) + EPS) * w
    return xn * (1.0 + scale) + shift


def _prologue_kernel(xc_ref, xl_ref, w_ref, sh_ref, sc_ref, u_ref, *, ctx_tiles):
    x = jnp.where(pl.program_id(0) < ctx_tiles, xc_ref[...], xl_ref[...])
    u_ref[...] = _norm_mod(x, w_ref[...], sh_ref[0], sc_ref[0]).astype(BF16)


def _prologue(h_ctx, h_lat, w, shift, scale, dec_seq):
    n_ctx, d = h_ctx.shape
    t = n_ctx + h_lat.shape[0]
    tm = _TILES["norm_rows"]
    ctx_tiles = n_ctx // tm
    seg = lambda i: (_seg(i, tm, n_ctx, dec_seq), 0, 0)
    need = 2 * (2 * tm * d * 4 + tm * d * 2) + 6 * tm * d * 4
    return pl.pallas_call(
        functools.partial(_prologue_kernel, ctx_tiles=ctx_tiles),
        grid=(t // tm,),
        in_specs=[
            pl.BlockSpec((tm, d), lambda i: (jnp.minimum(i, ctx_tiles - 1), 0)),
            pl.BlockSpec((tm, d), lambda i: (jnp.maximum(i - ctx_tiles, 0), 0)),
            pl.BlockSpec((1, d), lambda i: (0, 0)),
            pl.BlockSpec((1, 1, d), seg),
            pl.BlockSpec((1, 1, d), seg),
        ],
        out_specs=pl.BlockSpec((tm, d), lambda i: (i, 0)),
        out_shape=jax.ShapeDtypeStruct((t, d), BF16),
        compiler_params=_cparams(("arbitrary",), need),
        name="prologue",
    )(h_ctx, h_lat, w, shift, scale)


def _mm_kernel(a_ref, b_ref, o_ref):
    o_ref[...] = jnp.dot(a_ref[...], b_ref[...], preferred_element_type=F32).astype(o_ref.dtype)


def _matmul(a, b, out_dtype, b_col0=0, n=None, bn=None):
    m, k = a.shape
    n = b.shape[1] if n is None else n
    bm = min(_TILES["proj"][0], m)
    bn = min(_TILES["proj"][1] if bn is None else bn, n)
    j0 = b_col0 // bn
    osz = jnp.dtype(out_dtype).itemsize
    need = 2 * (bm * k * 2 + k * bn * 2 + bm * bn * osz) + bm * bn * 4
    return pl.pallas_call(
        _mm_kernel,
        grid=(m // bm, n // bn),
        in_specs=[
            pl.BlockSpec((bm, k), lambda i, j: (i, 0)),
            pl.BlockSpec((k, bn), lambda i, j: (0, j + j0)),
        ],
        out_specs=pl.BlockSpec((bm, bn), lambda i, j: (i, j)),
        out_shape=jax.ShapeDtypeStruct((m, n), out_dtype),
        compiler_params=_cparams(("arbitrary", "arbitrary"), need + 4 * 2**20),
        name="matmul",
    )(a, b)


def _mm_group_kernel(a_ref, b_ref, o_ref):
    o_ref[...] = jnp.dot(a_ref[...], b_ref[0], preferred_element_type=F32).astype(o_ref.dtype)


def _matmul_grouped(a, b, out_dtype):
    m = a.shape[0]
    groups, k, n = b.shape
    bm, bn = min(_TILES["proj"][0], m), min(_TILES["proj"][1], n)
    nj = n // bn
    osz = jnp.dtype(out_dtype).itemsize
    need = 2 * (bm * k * 2 + k * bn * 2 + bm * bn * osz) + bm * bn * 4
    return pl.pallas_call(
        _mm_group_kernel,
        grid=(groups, m // bm, nj),
        in_specs=[
            pl.BlockSpec((bm, k), lambda g, i, j: (i, g)),
            pl.BlockSpec((1, k, bn), lambda g, i, j: (g, 0, j)),
        ],
        out_specs=pl.BlockSpec((bm, bn), lambda g, i, j: (i, g * nj + j)),
        out_shape=jax.ShapeDtypeStruct((m, groups * n), out_dtype),
        compiler_params=_cparams(("arbitrary", "arbitrary", "arbitrary"), need + 4 * 2**20),
        name="matmul_grouped",
    )(a, b)


def _mm_res_kernel(*refs, nk, norm_dim):
    if norm_dim:
        a_ref, b_ref, h_ref, g_ref, ssq_ref, o_ref, acc_ref = refs
    else:
        a_ref, b_ref, h_ref, g_ref, o_ref, acc_ref = refs
    k = pl.program_id(2)

    @pl.when(k == 0)
    def _():
        acc_ref[...] = jnp.zeros_like(acc_ref)

    acc_ref[...] += jnp.dot(a_ref[...], b_ref[...], preferred_element_type=F32)

    @pl.when(k == nk - 1)
    def _():
        acc = acc_ref[...]
        if norm_dim:
            ms = jnp.sum(ssq_ref[...], axis=1, keepdims=True) * (1.0 / norm_dim)
            acc = acc * lax.rsqrt(ms + EPS)
        o_ref[...] = h_ref[...] + g_ref[0] * acc


def _matmul_residual(a, b, h, gate, seg0, seg_rows, ssq=None):
    m, kdim = a.shape
    n = b.shape[1]
    bm, bn, bk = (min(t, s) for t, s in zip(_TILES["out"], (m, n, kdim)))
    nk = kdim // bk
    need = 2 * (bm * bk * 2 + bk * bn * 2 + 2 * bm * bn * 4) + 2 * bm * bn * 4
    in_specs = [
        pl.BlockSpec((bm, bk), lambda i, j, k: (i, k)),
        pl.BlockSpec((bk, bn), lambda i, j, k: (k, j)),
        pl.BlockSpec((bm, bn), lambda i, j, k: (i, j)),
        pl.BlockSpec((1, 1, bn), lambda i, j, k: (seg0 + (i * bm) // seg_rows, 0, j)),
    ]
    args = [a, b, h, gate]
    if ssq is not None:
        in_specs.append(pl.BlockSpec((bm, ssq.shape[1]), lambda i, j, k: (i, 0)))
        args.append(ssq)
        need += 2 * bm * ssq.shape[1] * 4
    return pl.pallas_call(
        functools.partial(_mm_res_kernel, nk=nk, norm_dim=kdim if ssq is not None else 0),
        grid=(m // bm, n // bn, nk),
        in_specs=in_specs,
        out_specs=pl.BlockSpec((bm, bn), lambda i, j, k: (i, j)),
        out_shape=jax.ShapeDtypeStruct((m, n), F32),
        scratch_shapes=[pltpu.VMEM((bm, bn), F32)],
        compiler_params=_cparams(("arbitrary", "arbitrary", "arbitrary"), need + 4 * 2**20),
        name="matmul_residual",
    )(*args)


def _window_members(pos_t, pos_s, w):
    lo = pos_t - w // 2
    return (pos_s >= lo) & (pos_s < lo + w)


def _pool_membership(seq, grid_w):
    t = jnp.arange(seq, dtype=jnp.int32)[:, None]
    s = jnp.arange(seq, dtype=jnp.int32)[None, :]
    mats = []
    for w in POOL_WINDOWS:
        if grid_w is None:
            m = _window_members(t, s, w)
        else:
            m = _window_members(t // grid_w, s // grid_w, w) & _window_members(t % grid_w, s % grid_w, w)
        mats.append(m)
    return jnp.stack(mats).astype(BF16)


def _poolmix_kernel(y_ref, z_ref, p_ref, b_ref, sc_ref, o_ref, *, seq, nseq, rows_step):
    bias = b_ref[...]
    scale = sc_ref[...]
    for s in range(nseq):
        y_seq = y_ref[pl.ds(s * seq, seq), :]
        for r0 in range(0, seq, rows_step):
            member = p_ref[0, pl.ds(r0, rows_step), :]
            cnt = jnp.sum(member.astype(F32), axis=1, keepdims=True)
            rows = pl.ds(s * seq + r0, rows_step)
            tot = jnp.dot(member, y_seq, preferred_element_type=F32)
            mixed = tot / cnt - y_ref[rows, :].astype(F32) + bias
            z = z_ref[rows, :].astype(F32)
            o_ref[rows, :] = (mixed * scale * _silu(z)).astype(BF16)


def _poolmix(y, z, member, grp_b, scale, row0, n_rows, seq, tile):
    e = y.shape[1]
    gw = e // len(POOL_WINDOWS)
    nseq, cb = tile[0], min(tile[1], gw)
    rows = seq * nseq
    r0 = row0 // rows
    rows_step = min(seq, 512)
    need = 2 * (3 * rows * cb * 2 + seq * seq * 2) + 6 * rows_step * cb * 4 + rows_step * seq * 4
    return pl.pallas_call(
        functools.partial(_poolmix_kernel, seq=seq, nseq=nseq, rows_step=rows_step),
        grid=(e // cb, n_rows // rows),
        in_specs=[
            pl.BlockSpec((rows, cb), lambda c, i: (i + r0, c)),
            pl.BlockSpec((rows, cb), lambda c, i: (i + r0, c)),
            pl.BlockSpec((1, seq, seq), lambda c, i: (c * cb // gw, 0, 0)),
            pl.BlockSpec((1, cb), lambda c, i: (0, c)),
            pl.BlockSpec((1, cb), lambda c, i: (0, c)),
        ],
        out_specs=pl.BlockSpec((rows, cb), lambda c, i: (i, c)),
        out_shape=jax.ShapeDtypeStruct((n_rows, e), BF16),
        compiler_params=_cparams(("arbitrary", "arbitrary"), need + 8 * 2**20),
        name="poolmix",
    )(y, z, member, grp_b.reshape(1, e), scale.reshape(1, e))


def _conv_kernel(x_ref, w_ref, b_ref, o_ref, *, seq):
    x = x_ref[...].astype(F32)
    rows, cb = x.shape
    pos = lax.broadcasted_iota(jnp.int32, (rows, cb), 0) & (seq - 1)
    half = CONV_W // 2
    acc = b_ref[...] + w_ref[pl.ds(half, 1), :] * x
    for k in range(CONV_W):
        d = k - half
        if d == 0:
            continue
        shifted = pltpu.roll(x, (-d) % rows, axis=0)
        valid = (pos + d >= 0) & (pos + d < seq)
        acc = acc + w_ref[pl.ds(k, 1), :] * jnp.where(valid, shifted, 0.0)
    o_ref[...] = _silu(acc).astype(BF16)


def _conv_silu(proj, col0, n_cols, conv_w, conv_b, row0, n_rows, seq, rows):
    assert seq & (seq - 1) == 0 and rows % seq == 0
    cb = _TILES["conv_cols"]
    r0 = row0 // rows
    c0 = col0 // cb
    need = 2 * (2 * rows * cb * 2) + 8 * rows * cb * 4
    return pl.pallas_call(
        functools.partial(_conv_kernel, seq=seq),
        grid=(n_rows // rows, n_cols // cb),
        in_specs=[
            pl.BlockSpec((rows, cb), lambda i, c: (i + r0, c + c0)),
            pl.BlockSpec((CONV_W, cb), lambda i, c: (0, c)),
            pl.BlockSpec((1, cb), lambda i, c: (0, c)),
        ],
        out_specs=pl.BlockSpec((rows, cb), lambda i, c: (i, c)),
        out_shape=jax.ShapeDtypeStruct((n_rows, n_cols), BF16),
        compiler_params=_cparams(("arbitrary", "arbitrary"), need + 4 * 2**20),
        name="conv_silu",
    )(proj, conv_w, conv_b.reshape(1, n_cols))


_LOG2E = 1.4426950408889634


def _split3(v):
    hi = v.astype(BF16)
    r1 = v - hi.astype(F32)
    mid = r1.astype(BF16)
    lo = (r1 - mid.astype(F32)).astype(BF16)
    return hi, mid, lo


def _pack3(v, stride):
    hi, mid, lo = _split3(v)
    packed = hi.astype(F32) + pltpu.roll(mid.astype(F32), stride, axis=1) + pltpu.roll(lo.astype(F32), 2 * stride, axis=1)
    return packed.astype(BF16)


def _ssd_kernel(*refs, nc, hpg, d_inner, has_h0, want_state):
    it = iter(refs)
    x_ref, b_ref, c_ref, dt_ref, dtb_ref, alog_ref, dskip_ref, z_ref, gw_ref = (next(it) for _ in range(9))
    h0_ref = next(it) if has_h0 else None
    yg_ref, ssq_ref = next(it), next(it)
    st_ref = next(it) if want_state else None
    h_scr, y_scr, cum_scr, w3_scr, rowp_scr, bt_scr, s_scr, cd_scr = (next(it) for _ in range(8))

    q = CHUNK
    width = hpg * HEAD_DIM
    stride3 = 2 * hpg
    assert q == D_STATE == LANES and hpg % 2 == 0 and 2 * HEAD_DIM == LANES and 3 * stride3 <= LANES
    g = pl.program_id(1)
    shift0 = (LANES - hpg * g) % LANES
    shifts = (shift0, (shift0 + hpg) % LANES)

    lane = lax.broadcasted_iota(jnp.int32, (q, LANES), 1)
    lane1 = lax.broadcasted_iota(jnp.int32, (1, LANES), 1)
    row_i = lax.broadcasted_iota(jnp.int32, (q, q), 0)
    col_i = lax.broadcasted_iota(jnp.int32, (q, q), 1)
    masks = (row_i >= col_i, row_i <= col_i)
    tri_lower = masks[0].astype(BF16)
    e_row = lax.broadcasted_iota(jnp.int32, (LANES, width), 0)
    e_head = lax.broadcasted_iota(jnp.int32, (LANES, width), 1) // HEAD_DIM
    expand = tuple(((e_row % stride3 == d * hpg + e_head) & (e_row < 3 * stride3)).astype(BF16) for d in range(2))
    first_half = lane < HEAD_DIM

    def pick(v0, v1, other):
        ln = lane if v0.shape[0] == q else lane1
        return jnp.where(ln < hpg, v0, jnp.where(ln < 2 * hpg, v1, other))

    neg_a = pick(*(-jnp.exp(pltpu.roll(alog_ref[pl.ds(d, 1), :], shifts[d], axis=1)) for d in range(2)), 0.0)
    dt_bias = pick(*(pltpu.roll(dtb_ref[pl.ds(d, 1), :], shifts[d], axis=1) for d in range(2)), 0.0)

    for d in range(2):
        if has_h0:
            h_scr[d] = jnp.transpose(h0_ref[0, d])
        else:
            h_scr[d] = jnp.zeros((D_STATE, width), F32)

    def prepare(c, carry):
        rows = pl.ds(pl.multiple_of(c * q, q), q)
        dt_raw = pick(*(pltpu.roll(dt_ref[rows, pl.ds(d * LANES, LANES)], shifts[d], axis=1) for d in range(2)), 0.0)
        dt = jnp.where(lane < 2 * hpg, jax.nn.softplus(dt_raw + dt_bias), 0.0)
        a = dt * neg_a
        p3 = jnp.dot(tri_lower, _pack3(a, stride3), preferred_element_type=F32)
        prefix = jnp.where(lane < stride3, p3 + pltpu.roll(p3, LANES - stride3, axis=1) + pltpu.roll(p3, LANES - 2 * stride3, axis=1), 0.0)
        total = prefix[q - 1:q, :]
        cum = jnp.where(lane < hpg, prefix, total - prefix + a)
        cum_scr[rows, :] = cum * _LOG2E
        w3_scr[rows, :] = _pack3(dt * jnp.exp(total - cum), stride3)
        rowp_t = jnp.transpose((cum - jnp.log(dt)) * _LOG2E)
        rowp_scr[pl.ds(pl.multiple_of(c * stride3, stride3), stride3), :] = rowp_t[:stride3, :]
        bc = b_ref[rows, :]
        bt_scr[rows, :] = jnp.transpose(bc.astype(F32)).astype(BF16)
        s_scr[rows, :] = lax.dot_general(c_ref[rows, :], bc, (((1,), (1,)), ((), ())), preferred_element_type=F32).astype(BF16)
        cd3 = _pack3(jnp.broadcast_to(jnp.where(lane1 < stride3, jnp.exp(total), 0.0), (8, LANES)), stride3)
        for d in range(2):
            cd_scr[pl.ds(d * nc + c, 1), :] = jnp.dot(cd3, expand[d], preferred_element_type=F32)[:1, :]
        y_scr[rows, :] = x_ref[rows, :].astype(F32) * dskip_ref[...]
        return carry

    lax.fori_loop(0, nc, prepare, 0, unroll=2)

    def scan(i, carry):
        for d in range(2):
            c = i if d == 0 else nc - 1 - i
            rows = pl.ds(pl.multiple_of(c * q, q), q)
            cum = cum_scr[rows, :]
            scores = s_scr[rows, :]
            cc = c_ref[rows, :]
            h_in = h_scr[d]
            h_bf = h_in.astype(BF16)
            for p in range(hpg // 2):
                lanes = pl.ds(p * LANES, LANES)
                rhs = jnp.concatenate([x_ref[rows, lanes], h_bf[:, p * LANES:(p + 1) * LANES]], axis=0)
                halves = []
                for r in (2 * p, 2 * p + 1):
                    col = jnp.broadcast_to(cum[:, d * hpg + r:d * hpg + r + 1], (q, q))
                    rowp = rowp_scr[pl.ds(c * stride3 + d * hpg + r, 1), :]
                    decay = jnp.exp2(jnp.where(masks[d], col - rowp, -jnp.inf)).astype(BF16)
                    lhs = jnp.concatenate([scores * decay, cc * jnp.exp2(col).astype(BF16)], axis=1)
                    halves.append(jnp.dot(lhs, rhs, preferred_element_type=F32))
                y_scr[rows, lanes] += jnp.where(first_half, halves[0], halves[1])
            w_exp = jnp.dot(w3_scr[rows, :], expand[d], preferred_element_type=F32)
            xd = (x_ref[rows, :].astype(F32) * w_exp).astype(BF16)
            states = jnp.dot(bt_scr[rows, :], xd, preferred_element_type=F32)
            h_scr[d] = h_in * cd_scr[pl.ds(d * nc + c, 1), :] + states
        return carry

    lax.fori_loop(0, nc, scan, 0)

    def gate(c, carry):
        rows = pl.ds(pl.multiple_of(c * q, q), q)
        t = y_scr[rows, :] * _silu(z_ref[rows, :].astype(F32))
        yg_ref[rows, :] = (t * gw_ref[...]).astype(BF16)
        t2 = t * t
        ssq_ref[rows, :] = sum(t2[:, k * LANES:(k + 1) * LANES] for k in range(width // LANES))
        return carry

    lax.fori_loop(0, nc, gate, 0)

    if want_state:
        for d in range(2):
            st_ref[0, d] = jnp.transpose(h_scr[d])


def _ssd(xbc, proj, dt_raw, dt_bias, a_log, d_skip_exp, gnorm_w, h0, row0, n_seq, seq, n_heads, want_state):
    d_inner = n_heads * HEAD_DIM
    hpg = n_heads // N_GROUPS
    width = hpg * HEAD_DIM
    nc = seq // CHUNK
    r0 = row0 // seq
    b0 = d_inner // D_STATE
    has_h0 = h0 is not None
    in_specs = [
        pl.BlockSpec((seq, width), lambda s, g: (s, g)),
        pl.BlockSpec((seq, D_STATE), lambda s, g: (s, b0 + g)),
        pl.BlockSpec((seq, D_STATE), lambda s, g: (s, b0 + N_GROUPS + g)),
        pl.BlockSpec((seq, 2 * n_heads), lambda s, g: (s + r0, 0)),
        pl.BlockSpec((2, n_heads), lambda s, g: (0, 0)),
        pl.BlockSpec((2, n_heads), lambda s, g: (0, 0)),
        pl.BlockSpec((1, width), lambda s, g: (0, g)),
        pl.BlockSpec((seq, width), lambda s, g: (s + r0, g)),
        pl.BlockSpec((1, width), lambda s, g: (0, g)),
    ]
    args = [xbc, xbc, xbc, dt_raw, dt_bias, a_log, d_skip_exp, proj, gnorm_w.reshape(1, d_inner)]
    if has_h0:
        in_specs.append(pl.BlockSpec((1, 2, width, D_STATE), lambda s, g: (s, 0, g, 0)))
        args.append(h0)
    out_specs = [pl.BlockSpec((seq, width), lambda s, g: (s, g)), pl.BlockSpec((seq, LANES), lambda s, g: (s, g))]
    out_shape = [jax.ShapeDtypeStruct((n_seq * seq, d_inner), BF16), jax.ShapeDtypeStruct((n_seq * seq, N_GROUPS * LANES), F32)]
    if want_state:
        out_specs.append(pl.BlockSpec((1, 2, width, D_STATE), lambda s, g: (s, 0, g, 0)))
        out_shape.append(jax.ShapeDtypeStruct((n_seq, 2, d_inner, D_STATE), F32))
    scratch = [
        pltpu.VMEM((2, D_STATE, width), F32),
        pltpu.VMEM((seq, width), F32),
        pltpu.VMEM((seq, LANES), F32),
        pltpu.VMEM((seq, LANES), BF16),
        pltpu.VMEM((nc * 2 * hpg, CHUNK), F32),
        pltpu.VMEM((seq, CHUNK), BF16),
        pltpu.VMEM((seq, CHUNK), BF16),
        pltpu.VMEM((max(2 * nc, 8), width), F32),
    ]
    need = (2 * (2 * seq * width * 2 + 2 * seq * D_STATE * 2 + seq * 2 * n_heads * 4 + seq * width * 2 + seq * LANES * 4)
            + (4 if has_h0 else 0) * width * D_STATE * 4 + (4 if want_state else 0) * width * D_STATE * 4
            + 2 * D_STATE * width * 4 + seq * width * 4 + seq * LANES * 12 + 16 * CHUNK * width * 4)
    return pl.pallas_call(
        functools.partial(_ssd_kernel, nc=nc, hpg=hpg, d_inner=d_inner, has_h0=has_h0, want_state=want_state),
        grid=(n_seq, N_GROUPS),
        in_specs=in_specs,
        out_specs=out_specs,
        out_shape=out_shape,
        scratch_shapes=scratch,
        compiler_params=_cparams(("arbitrary", "arbitrary"), need + 6 * 2**20),
        name="ssd_scan",
    )(*args)


def _final_norm_kernel(x_ref, w_ref, o_ref):
    x = x_ref[...]
    o_ref[...] = x * lax.rsqrt(jnp.mean(x * x, axis=-1, keepdims=True) + EPS) * w_ref[...]


def _final_norm(h, w):
    rows, d = h.shape
    tm = _TILES["norm_rows"]
    need = 2 * (2 * tm * d * 4) + 4 * tm * d * 4
    return pl.pallas_call(
        _final_norm_kernel,
        grid=(rows // tm,),
        in_specs=[pl.BlockSpec((tm, d), lambda i: (i, 0)), pl.BlockSpec((1, d), lambda i: (0, 0))],
        out_specs=pl.BlockSpec((tm, d), lambda i: (i, 0)),
        out_shape=jax.ShapeDtypeStruct((rows, d), F32),
        compiler_params=_cparams(("arbitrary",), need),
        name="final_norm",
    )(h, w.reshape(1, d))


def kernel(x_prompt, x_sample, state_ssd, c, c_ctx, ada_w, ada_b, norm_w, pool_in_w, pool_grp_w, pool_grp_b, pool_scale, pool_out_w, ssd_in_w, ssd_conv_w, ssd_conv_b, ssd_dt_bias, ssd_A_log, ssd_D, ssd_norm_w, ssd_out_w, final_norm_w):
    batch, seq, d = x_prompt.shape
    dec_batch, dec_seq, _ = x_sample.shape
    n_ctx, n_lat = batch * seq, dec_batch * dec_seq
    depth = ada_w.shape[0]
    n_heads = ssd_A_log.shape[-1]
    d_inner = n_heads * HEAD_DIM
    e_pool = pool_out_w.shape[1]

    c_rows = jnp.concatenate([c_ctx[None, :], c, jnp.zeros((8 - 1 - dec_batch, d), F32)], axis=0)
    mod = _ada_mod(c_rows, ada_w, ada_b).reshape(depth, 8, 3, d)
    nseg = 1 + dec_batch
    part = lambda layer, which: mod[layer, :nseg, which][:, None, :]

    h_ctx = x_prompt.reshape(n_ctx, d)
    h_lat = x_sample.reshape(n_lat, d)
    new_states = []
    for layer in range(depth):
        j = layer // 2
        shift, scale, gate = (part(layer, k) for k in range(3))
        u = _prologue(h_ctx, h_lat, norm_w[layer].reshape(1, d), shift, scale, dec_seq)

        if layer % 2 == 0:
            w_in = pool_in_w[j].astype(BF16)
            w_fold = _matmul_grouped(w_in, pool_grp_w[j].astype(BF16), BF16)
            y = _matmul(u, w_fold, BF16)
            z = _matmul(u, w_in, BF16, b_col0=e_pool, n=e_pool)
            v_ctx = _poolmix(y, z, _pool_membership(seq, None), pool_grp_b[j], pool_scale[j],
                             0, n_ctx, seq, _TILES["pool_ctx"])
            v_lat = _poolmix(y, z, _pool_membership(dec_seq, GRID_W), pool_grp_b[j], pool_scale[j],
                             n_ctx, n_lat, dec_seq, _TILES["pool_lat"])
            w_out = pool_out_w[j].astype(BF16)
            ssq_ctx = ssq_lat = None
        else:
            w_in = ssd_in_w[j].astype(BF16)
            n_main = 2 * d_inner + 2 * N_GROUPS * D_STATE
            proj = _matmul(u, w_in, BF16, n=n_main)
            dt_raw = _matmul(u, w_in, F32, b_col0=n_main, n=2 * n_heads, bn=2 * n_heads)
            n_conv = n_main - d_inner
            d_exp = jnp.repeat(ssd_D[j], HEAD_DIM).reshape(1, d_inner)
            conv = lambda row0, n_rows, sq, rows: _conv_silu(proj, d_inner, n_conv, ssd_conv_w[j], ssd_conv_b[j], row0, n_rows, sq, rows)
            ssd = functools.partial(_ssd, proj=proj, dt_raw=dt_raw, dt_bias=ssd_dt_bias[j], a_log=ssd_A_log[j],
                                    d_skip_exp=d_exp, gnorm_w=ssd_norm_w[j], n_heads=n_heads)
            v_ctx, ssq_ctx, st = ssd(conv(0, n_ctx, seq, 4 * seq), h0=None, row0=0, n_seq=batch, seq=seq, want_state=True)
            h0 = state_ssd[:, j].reshape(dec_batch, 2, d_inner, D_STATE)
            v_lat, ssq_lat = ssd(conv(n_ctx, n_lat, dec_seq, dec_seq), h0=h0, row0=n_ctx, n_seq=dec_batch, seq=dec_seq, want_state=False)
            new_states.append(st.reshape(batch, 2, n_heads, HEAD_DIM, D_STATE))
            w_out = ssd_out_w[j].astype(BF16)

        h_ctx = _matmul_residual(v_ctx, w_out, h_ctx, gate, 0, n_ctx, ssq_ctx)
        h_lat = _matmul_residual(v_lat, w_out, h_lat, gate, 1, dec_seq, ssq_lat)

    y_prompt = _final_norm(h_ctx, final_norm_w).reshape(batch, seq, d)
    y_sample = _final_norm(h_lat, final_norm_w).reshape(dec_batch, dec_seq, d)
    return (y_prompt, y_sample, jnp.stack(new_states, axis=1))
```

```python
import functools

import jax
import jax.numpy as jnp
from jax import lax
from jax.experimental import pallas as pl
from jax.experimental.pallas import tpu as pltpu

F32 = jnp.float32
BF16 = jnp.bfloat16

EPS = 1e-6
POOL_WINDOWS = (2, 4, 8, 16)
GRID_W = 64
HEAD_DIM = 64
D_STATE = 128
N_GROUPS = 8
CONV_W = 7
CHUNK = 128
LANES = 128

V7X_VMEM_BYTES = 64 * 2**20
_VMEM_HEADROOM = 6 * 2**20

_TILES = dict(
    ada_bn=512,
    norm_rows=256,
    proj=(1024, 1024),
    out=(512, 512, 8192),
    pool_ctx=(4, 1024),
    pool_lat=(1, 512),
    conv_cols=512,
    ssd_seqs_ctx=2,
    ssd_seqs_lat=1,
)


def _cparams(semantics, vmem_need):
    limit = min(max(int(vmem_need), 16 * 2**20), V7X_VMEM_BYTES - _VMEM_HEADROOM)
    return pltpu.CompilerParams(dimension_semantics=semantics, vmem_limit_bytes=limit)


def _silu(t):
    return t * jax.nn.sigmoid(t)


def _seg(tile, rows_per_tile, n_ctx, dec_seq):
    start = tile * rows_per_tile
    return jnp.where(start < n_ctx, 0, 1 + jnp.maximum(start - n_ctx, 0) // dec_seq)


def _ada_kernel(c_ref, w_ref, b_ref, o_ref):
    s = _silu(c_ref[...]).astype(BF16)
    o_ref[0] = jnp.dot(s, w_ref[0].astype(BF16), preferred_element_type=F32) + b_ref[0]


def _ada_mod(c_rows, ada_w, ada_b):
    depth, d, n3 = ada_w.shape
    rows = c_rows.shape[0]
    bn = min(_TILES["ada_bn"], n3)
    need = 2 * (d * bn * 4) + d * bn * 2 + 4 * rows * d * 4
    return pl.pallas_call(
        _ada_kernel,
        grid=(depth, n3 // bn),
        in_specs=[
            pl.BlockSpec((rows, d), lambda l, j: (0, 0)),
            pl.BlockSpec((1, d, bn), lambda l, j: (l, 0, j)),
            pl.BlockSpec((1, 1, bn), lambda l, j: (l, 0, j)),
        ],
        out_specs=pl.BlockSpec((1, rows, bn), lambda l, j: (l, 0, j)),
        out_shape=jax.ShapeDtypeStruct((depth, rows, n3), F32),
        compiler_params=_cparams(("arbitrary", "arbitrary"), need + 8 * 2**20),
        name="ada_mod",
    )(c_rows, ada_w, ada_b.reshape(depth, 1, n3))


def _norm_mod(x, w, shift, scale):
    xn = x * lax.rsqrt(jnp.mean(x * x, axis=-1, keepdims=True) + EPS) * w
    return xn * (1.0 + scale) + shift


def _prologue_kernel(xc_ref, xl_ref, w_ref, sh_ref, sc_ref, u_ref, *, ctx_tiles):
    x = jnp.where(pl.program_id(0) < ctx_tiles, xc_ref[...], xl_ref[...])
    u_ref[...] = _norm_mod(x, w_ref[...], sh_ref[0], sc_ref[0]).astype(BF16)


def _prologue(h_ctx, h_lat, w, shift, scale, dec_seq):
    n_ctx, d = h_ctx.shape
    t = n_ctx + h_lat.shape[0]
    tm = _TILES["norm_rows"]
    ctx_tiles = n_ctx // tm
    seg = lambda i: (_seg(i, tm, n_ctx, dec_seq), 0, 0)
    need = 2 * (2 * tm * d * 4 + tm * d * 2) + 6 * tm * d * 4
    return pl.pallas_call(
        functools.partial(_prologue_kernel, ctx_tiles=ctx_tiles),
        grid=(t // tm,),
        in_specs=[
            pl.BlockSpec((tm, d), lambda i: (jnp.minimum(i, ctx_tiles - 1), 0)),
            pl.BlockSpec((tm, d), lambda i: (jnp.maximum(i - ctx_tiles, 0), 0)),
            pl.BlockSpec((1, d), lambda i: (0, 0)),
            pl.BlockSpec((1, 1, d), seg),
            pl.BlockSpec((1, 1, d), seg),
        ],
        out_specs=pl.BlockSpec((tm, d), lambda i: (i, 0)),
        out_shape=jax.ShapeDtypeStruct((t, d), BF16),
        compiler_params=_cparams(("arbitrary",), need),
        name="prologue",
    )(h_ctx, h_lat, w, shift, scale)


def _mm_kernel(a_ref, b_ref, o_ref):
    o_ref[...] = jnp.dot(a_ref[...], b_ref[...], preferred_element_type=F32).astype(o_ref.dtype)


def _matmul(a, b, out_dtype, b_col0=0, n=None, bn=None):
    m, k = a.shape
    n = b.shape[1] if n is None else n
    bm = min(_TILES["proj"][0], m)
    bn = min(_TILES["proj"][1] if bn is None else bn, n)
    j0 = b_col0 // bn
    osz = jnp.dtype(out_dtype).itemsize
    need = 2 * (bm * k * 2 + k * bn * 2 + bm * bn * osz) + bm * bn * 4
    return pl.pallas_call(
        _mm_kernel,
        grid=(m // bm, n // bn),
        in_specs=[
            pl.BlockSpec((bm, k), lambda i, j: (i, 0)),
            pl.BlockSpec((k, bn), lambda i, j: (0, j + j0)),
        ],
        out_specs=pl.BlockSpec((bm, bn), lambda i, j: (i, j)),
        out_shape=jax.ShapeDtypeStruct((m, n), out_dtype),
        compiler_params=_cparams(("arbitrary", "arbitrary"), need + 4 * 2**20),
        name="matmul",
    )(a, b)


def _mm_group_kernel(a_ref, b_ref, o_ref):
    o_ref[...] = jnp.dot(a_ref[...], b_ref[0], preferred_element_type=F32).astype(o_ref.dtype)


def _matmul_grouped(a, b, out_dtype):
    m = a.shape[0]
    groups, k, n = b.shape
    bm, bn = min(_TILES["proj"][0], m), min(_TILES["proj"][1], n)
    nj = n // bn
    osz = jnp.dtype(out_dtype).itemsize
    need = 2 * (bm * k * 2 + k * bn * 2 + bm * bn * osz) + bm * bn * 4
    return pl.pallas_call(
        _mm_group_kernel,
        grid=(groups, m // bm, nj),
        in_specs=[
            pl.BlockSpec((bm, k), lambda g, i, j: (i, g)),
            pl.BlockSpec((1, k, bn), lambda g, i, j: (g, 0, j)),
        ],
        out_specs=pl.BlockSpec((bm, bn), lambda g, i, j: (i, g * nj + j)),
        out_shape=jax.ShapeDtypeStruct((m, groups * n), out_dtype),
        compiler_params=_cparams(("arbitrary", "arbitrary", "arbitrary"), need + 4 * 2**20),
        name="matmul_grouped",
    )(a, b)


def _mm_res_kernel(*refs, nk, norm_dim):
    a_ref, b_ref, h_ref, g_ref = refs[:4]
    ssq_ref = refs[4] if norm_dim else None
    o_ref = refs[5 if norm_dim else 4]

    def finish(acc):
        if norm_dim:
            ms = jnp.sum(ssq_ref[...], axis=1, keepdims=True) * (1.0 / norm_dim)
            acc = acc * lax.rsqrt(ms + EPS)
        o_ref[...] = h_ref[...] + g_ref[0] * acc

    if nk == 1:
        finish(jnp.dot(a_ref[...], b_ref[...], preferred_element_type=F32))
        return
    acc_ref = refs[-1]
    k = pl.program_id(2)

    @pl.when(k == 0)
    def _():
        acc_ref[...] = jnp.zeros_like(acc_ref)

    acc_ref[...] += jnp.dot(a_ref[...], b_ref[...], preferred_element_type=F32)

    @pl.when(k == nk - 1)
    def _():
        finish(acc_ref[...])


def _matmul_residual(a, b, h, gate, seg0, seg_rows, ssq=None):
    m, kdim = a.shape
    n = b.shape[1]
    bm, bn, bk = (min(t, s) for t, s in zip(_TILES["out"], (m, n, kdim)))
    nk = kdim // bk
    need = 2 * (bm * bk * 2 + bk * bn * 2 + 2 * bm * bn * 4) + 2 * bm * bn * 4
    in_specs = [
        pl.BlockSpec((bm, bk), lambda i, j, k: (i, k)),
        pl.BlockSpec((bk, bn), lambda i, j, k: (k, j)),
        pl.BlockSpec((bm, bn), lambda i, j, k: (i, j)),
        pl.BlockSpec((1, 1, bn), lambda i, j, k: (seg0 + (i * bm) // seg_rows, 0, j)),
    ]
    args = [a, b, h, gate]
    if ssq is not None:
        in_specs.append(pl.BlockSpec((bm, ssq.shape[1]), lambda i, j, k: (i, 0)))
        args.append(ssq)
        need += 2 * bm * ssq.shape[1] * 4
    return pl.pallas_call(
        functools.partial(_mm_res_kernel, nk=nk, norm_dim=kdim if ssq is not None else 0),
        grid=(m // bm, n // bn, nk),
        in_specs=in_specs,
        out_specs=pl.BlockSpec((bm, bn), lambda i, j, k: (i, j)),
        out_shape=jax.ShapeDtypeStruct((m, n), F32),
        scratch_shapes=[pltpu.VMEM((bm, bn), F32)] if nk > 1 else [],
        compiler_params=_cparams(("arbitrary", "arbitrary", "arbitrary"), need + 4 * 2**20),
        name="matmul_residual",
    )(*args)


def _window_members(pos_t, pos_s, w):
    lo = pos_t - w // 2
    return (pos_s >= lo) & (pos_s < lo + w)


def _pool_membership(seq, grid_w):
    t = jnp.arange(seq, dtype=jnp.int32)[:, None]
    s = jnp.arange(seq, dtype=jnp.int32)[None, :]
    mats = []
    for w in POOL_WINDOWS:
        if grid_w is None:
            m = _window_members(t, s, w)
        else:
            m = _window_members(t // grid_w, s // grid_w, w) & _window_members(t % grid_w, s % grid_w, w)
        mats.append(m)
    return jnp.stack(mats).astype(BF16)


def _poolmix_kernel(y_ref, z_ref, p_ref, b_ref, sc_ref, o_ref, *, seq, nseq, rows_step, reaches, cols_per_group):
    bias = b_ref[...]
    scale = sc_ref[...]

    def mix(reach):
        for s in range(nseq):
            for r0 in range(0, seq, rows_step):
                k0, k1 = max(0, r0 - reach), min(seq, r0 + rows_step + reach)
                member = p_ref[0, pl.ds(r0, rows_step), pl.ds(k0, k1 - k0)]
                cnt = jnp.sum(member.astype(F32), axis=1, keepdims=True)
                rows = pl.ds(s * seq + r0, rows_step)
                tot = jnp.dot(member, y_ref[pl.ds(s * seq + k0, k1 - k0), :], preferred_element_type=F32)
                mixed = tot / cnt - y_ref[rows, :].astype(F32) + bias
                z = z_ref[rows, :].astype(F32)
                o_ref[rows, :] = (mixed * scale * _silu(z)).astype(BF16)

    if seq <= rows_step:
        mix(seq)
    else:
        group = pl.program_id(0) // cols_per_group
        for gi, reach in enumerate(reaches):
            pl.when(group == gi)(functools.partial(mix, reach))


def _poolmix(y, z, member, grp_b, scale, row0, n_rows, seq, tile, grid_w):
    e = y.shape[1]
    gw = e // len(POOL_WINDOWS)
    nseq, cb = tile[0], min(tile[1], gw)
    rows = seq * nseq
    r0 = row0 // rows
    rows_step = min(seq, 512)
    reaches = tuple(-(-(w // 2) * (grid_w or 1) // LANES) * LANES for w in POOL_WINDOWS)
    need = 2 * (3 * rows * cb * 2 + seq * seq * 2) + 6 * rows_step * cb * 4 + rows_step * seq * 4
    return pl.pallas_call(
        functools.partial(_poolmix_kernel, seq=seq, nseq=nseq, rows_step=rows_step, reaches=reaches, cols_per_group=gw // cb),
        grid=(e // cb, n_rows // rows),
        in_specs=[
            pl.BlockSpec((rows, cb), lambda c, i: (i + r0, c)),
            pl.BlockSpec((rows, cb), lambda c, i: (i + r0, c)),
            pl.BlockSpec((1, seq, seq), lambda c, i: (c * cb // gw, 0, 0)),
            pl.BlockSpec((1, cb), lambda c, i: (0, c)),
            pl.BlockSpec((1, cb), lambda c, i: (0, c)),
        ],
        out_specs=pl.BlockSpec((rows, cb), lambda c, i: (i, c)),
        out_shape=jax.ShapeDtypeStruct((n_rows, e), BF16),
        compiler_params=_cparams(("arbitrary", "arbitrary"), need + 8 * 2**20),
        name="poolmix",
    )(y, z, member, grp_b.reshape(1, e), scale.reshape(1, e))


def _conv_kernel(x_ref, w_ref, b_ref, o_ref, *, seq):
    x = x_ref[...].astype(F32)
    rows, cb = x.shape
    pos = lax.broadcasted_iota(jnp.int32, (rows, cb), 0) & (seq - 1)
    half = CONV_W // 2
    acc = b_ref[...] + w_ref[pl.ds(half, 1), :] * x
    for k in range(CONV_W):
        d = k - half
        if d == 0:
            continue
        shifted = pltpu.roll(x, (-d) % rows, axis=0)
        valid = (pos + d >= 0) & (pos + d < seq)
        acc = acc + w_ref[pl.ds(k, 1), :] * jnp.where(valid, shifted, 0.0)
    o_ref[...] = _silu(acc).astype(BF16)


def _conv_silu(proj, col0, n_cols, conv_w, conv_b, row0, n_rows, seq, rows):
    assert seq & (seq - 1) == 0 and rows % seq == 0
    cb = _TILES["conv_cols"]
    r0 = row0 // rows
    c0 = col0 // cb
    need = 2 * (2 * rows * cb * 2) + 8 * rows * cb * 4
    return pl.pallas_call(
        functools.partial(_conv_kernel, seq=seq),
        grid=(n_rows // rows, n_cols // cb),
        in_specs=[
            pl.BlockSpec((rows, cb), lambda i, c: (i + r0, c + c0)),
            pl.BlockSpec((CONV_W, cb), lambda i, c: (0, c)),
            pl.BlockSpec((1, cb), lambda i, c: (0, c)),
        ],
        out_specs=pl.BlockSpec((rows, cb), lambda i, c: (i, c)),
        out_shape=jax.ShapeDtypeStruct((n_rows, n_cols), BF16),
        compiler_params=_cparams(("arbitrary", "arbitrary"), need + 4 * 2**20),
        name="conv_silu",
    )(proj, conv_w, conv_b.reshape(1, n_cols))


_LOG2E = 1.4426950408889634


def _split3(v):
    hi = v.astype(BF16)
    r1 = v - hi.astype(F32)
    mid = r1.astype(BF16)
    lo = (r1 - mid.astype(F32)).astype(BF16)
    return hi, mid, lo


def _pack3(v, stride):
    hi, mid, lo = _split3(v)
    packed = hi.astype(F32) + pltpu.roll(mid.astype(F32), stride, axis=1) + pltpu.roll(lo.astype(F32), 2 * stride, axis=1)
    return packed.astype(BF16)


def _ssd_kernel(*refs, nb, nc, hpg, has_h0, want_state):
    it = iter(refs)
    x_ref, b_ref, c_ref, dt_ref, dtb_ref, alog_ref, dskip_ref, z_ref, gw_ref = (next(it) for _ in range(9))
    h0_ref = next(it) if has_h0 else None
    yg_ref, ssq_ref = next(it), next(it)
    st_ref = next(it) if want_state else None
    h_scr, y_scr, cum_scr, w3_scr, rowp_scr, bt_scr, s_scr, cd_scr = (next(it) for _ in range(8))

    q = CHUNK
    nct = nb * nc
    width = hpg * HEAD_DIM
    stride3 = 2 * hpg
    assert q == D_STATE == LANES and hpg % 2 == 0 and 2 * HEAD_DIM == LANES and 3 * stride3 <= LANES
    g = pl.program_id(1)
    shift0 = (LANES - hpg * g) % LANES
    shifts = (shift0, (shift0 + hpg) % LANES)

    lane = lax.broadcasted_iota(jnp.int32, (q, LANES), 1)
    lane_all = lax.broadcasted_iota(jnp.int32, (nct * q, LANES), 1)
    lane1 = lax.broadcasted_iota(jnp.int32, (1, LANES), 1)
    row_i = lax.broadcasted_iota(jnp.int32, (q, q), 0)
    col_i = lax.broadcasted_iota(jnp.int32, (q, q), 1)
    masks = (row_i >= col_i, row_i <= col_i)
    tri = tuple(m.astype(BF16) for m in masks)
    ones = jnp.ones((q, q), BF16)
    e_row = lax.broadcasted_iota(jnp.int32, (LANES, width), 0)
    e_head = lax.broadcasted_iota(jnp.int32, (LANES, width), 1) // HEAD_DIM
    expand = tuple(((e_row % stride3 == d * hpg + e_head) & (e_row < 3 * stride3)).astype(BF16) for d in range(2))
    first_half = lane < HEAD_DIM

    def pick(v0, v1, ln):
        return jnp.where(ln < hpg, v0, jnp.where(ln < 2 * hpg, v1, 0.0))

    def unpack3(p):
        s = p + pltpu.roll(p, LANES - stride3, axis=1) + pltpu.roll(p, LANES - 2 * stride3, axis=1)
        return jnp.where(lane_all < stride3, s, 0.0)

    for sl in range(nb):
        for d in range(2):
            if has_h0:
                h_scr[2 * sl + d] = jnp.transpose(h0_ref[sl, d])
            else:
                h_scr[2 * sl + d] = jnp.zeros((D_STATE, width), F32)

    neg_a = pick(*(-jnp.exp(pltpu.roll(alog_ref[pl.ds(d, 1), :], shifts[d], axis=1)) for d in range(2)), lane1)
    dt_bias = pick(*(pltpu.roll(dtb_ref[pl.ds(d, 1), :], shifts[d], axis=1) for d in range(2)), lane1)
    dt_raw = pick(*(pltpu.roll(dt_ref[:, pl.ds(d * LANES, LANES)], shifts[d], axis=1) for d in range(2)), lane_all)
    dt = jnp.where(lane_all < 2 * hpg, jax.nn.softplus(dt_raw + dt_bias), 0.0)
    a3 = _pack3(dt * neg_a, stride3)
    chunks = [a3[k * q:(k + 1) * q] for k in range(nct)]
    pfx, sfx, tot = (unpack3(jnp.concatenate([jnp.dot(m, ck, preferred_element_type=F32) for ck in chunks], axis=0))
                     for m in (tri[0], tri[1], ones))
    cum = jnp.where(lane_all < hpg, pfx, sfx)
    cum_scr[...] = cum * _LOG2E
    w3_scr[...] = _pack3(dt * jnp.exp(tot - cum), stride3)
    rowp = (cum - jnp.log(dt)) * _LOG2E
    decay8 = jnp.concatenate([jnp.exp(tot[k * q:k * q + 8]) for k in range(nct)], axis=0)
    lane8 = lax.broadcasted_iota(jnp.int32, (8 * nct, LANES), 1)
    cd3 = _pack3(jnp.where(lane8 < stride3, decay8, 0.0), stride3)
    for d in range(2):
        cd_scr[d] = jnp.dot(cd3, expand[d], preferred_element_type=F32)
    for k in range(nct):
        rows = pl.ds(k * q, q)
        rowp_scr[pl.ds(k * stride3, stride3), :] = jnp.transpose(rowp[k * q:(k + 1) * q])[:stride3, :]
        bc = b_ref[rows, :]
        bt_scr[rows, :] = jnp.transpose(bc.astype(F32)).astype(BF16)
        s_scr[rows, :] = lax.dot_general(c_ref[rows, :], bc, (((1,), (1,)), ((), ())), preferred_element_type=F32).astype(BF16)
    y_scr[...] = x_ref[...].astype(F32) * dskip_ref[...]

    def scan(i, carry):
        for sl in range(nb):
            for d in range(2):
                k = sl * nc + (i if d == 0 else nc - 1 - i)
                rows = pl.ds(pl.multiple_of(k * q, q), q)
                cum_c = cum_scr[rows, :]
                scores = s_scr[rows, :]
                cc = c_ref[rows, :]
                h_in = h_scr[2 * sl + d]
                h_bf = h_in.astype(BF16)
                for p in range(hpg // 2):
                    lanes = pl.ds(p * LANES, LANES)
                    rhs = jnp.concatenate([x_ref[rows, lanes], h_bf[:, p * LANES:(p + 1) * LANES]], axis=0)
                    halves = []
                    for r in (2 * p, 2 * p + 1):
                        col = jnp.broadcast_to(cum_c[:, d * hpg + r:d * hpg + r + 1], (q, q))
                        row = rowp_scr[pl.ds(k * stride3 + d * hpg + r, 1), :]
                        decay = jnp.exp2(jnp.where(masks[d], col - row, -jnp.inf)).astype(BF16)
                        lhs = jnp.concatenate([scores * decay, cc * jnp.exp2(col).astype(BF16)], axis=1)
                        halves.append(jnp.dot(lhs, rhs, preferred_element_type=F32))
                    y_scr[rows, lanes] += jnp.where(first_half, halves[0], halves[1])
                w_exp = jnp.dot(w3_scr[rows, :], expand[d], preferred_element_type=F32)
                xd = (x_ref[rows, :].astype(F32) * w_exp).astype(BF16)
                states = jnp.dot(bt_scr[rows, :], xd, preferred_element_type=F32)
                h_scr[2 * sl + d] = h_in * cd_scr[d, pl.ds(pl.multiple_of(k * 8, 8), 1), :] + states
        return carry

    lax.fori_loop(0, nc, scan, 0)

    def gate(k, carry):
        rows = pl.ds(pl.multiple_of(k * q, q), q)
        t = y_scr[rows, :] * _silu(z_ref[rows, :].astype(F32))
        yg_ref[rows, :] = (t * gw_ref[...]).astype(BF16)
        t2 = t * t
        ssq_ref[rows, :] = sum(t2[:, j * LANES:(j + 1) * LANES] for j in range(width // LANES))
        return carry

    lax.fori_loop(0, nct, gate, 0)

    if want_state:
        for sl in range(nb):
            for d in range(2):
                st_ref[sl, d] = jnp.transpose(h_scr[2 * sl + d])


def _ssd(xbc, proj, dt_raw, dt_bias, a_log, d_skip_exp, gnorm_w, h0, row0, n_seq, seq, n_heads, want_state, nb):
    d_inner = n_heads * HEAD_DIM
    hpg = n_heads // N_GROUPS
    width = hpg * HEAD_DIM
    nc = seq // CHUNK
    rows = nb * seq
    r0 = row0 // rows
    b0 = d_inner // D_STATE
    has_h0 = h0 is not None
    in_specs = [
        pl.BlockSpec((rows, width), lambda s, g: (s, g)),
        pl.BlockSpec((rows, D_STATE), lambda s, g: (s, b0 + g)),
        pl.BlockSpec((rows, D_STATE), lambda s, g: (s, b0 + N_GROUPS + g)),
        pl.BlockSpec((rows, 2 * n_heads), lambda s, g: (s + r0, 0)),
        pl.BlockSpec((2, n_heads), lambda s, g: (0, 0)),
        pl.BlockSpec((2, n_heads), lambda s, g: (0, 0)),
        pl.BlockSpec((1, width), lambda s, g: (0, g)),
        pl.BlockSpec((rows, width), lambda s, g: (s + r0, g)),
        pl.BlockSpec((1, width), lambda s, g: (0, g)),
    ]
    args = [xbc, xbc, xbc, dt_raw, dt_bias, a_log, d_skip_exp, proj, gnorm_w.reshape(1, d_inner)]
    if has_h0:
        in_specs.append(pl.BlockSpec((nb, 2, width, D_STATE), lambda s, g: (s, 0, g, 0)))
        args.append(h0)
    out_specs = [pl.BlockSpec((rows, width), lambda s, g: (s, g)), pl.BlockSpec((rows, LANES), lambda s, g: (s, g))]
    out_shape = [jax.ShapeDtypeStruct((n_seq * seq, d_inner), BF16), jax.ShapeDtypeStruct((n_seq * seq, N_GROUPS * LANES), F32)]
    if want_state:
        out_specs.append(pl.BlockSpec((nb, 2, width, D_STATE), lambda s, g: (s, 0, g, 0)))
        out_shape.append(jax.ShapeDtypeStruct((n_seq, 2, d_inner, D_STATE), F32))
    nct = nb * nc
    scratch = [
        pltpu.VMEM((2 * nb, D_STATE, width), F32),
        pltpu.VMEM((rows, width), F32),
        pltpu.VMEM((rows, LANES), F32),
        pltpu.VMEM((rows, LANES), BF16),
        pltpu.VMEM((nct * 2 * hpg, CHUNK), F32),
        pltpu.VMEM((rows, CHUNK), BF16),
        pltpu.VMEM((rows, CHUNK), BF16),
        pltpu.VMEM((2, 8 * nct, width), F32),
    ]
    need = (2 * (2 * rows * width * 2 + 2 * rows * D_STATE * 2 + rows * 2 * n_heads * 4 + rows * width * 2 + rows * LANES * 4)
            + ((4 if has_h0 else 0) + (4 if want_state else 0) + 2) * nb * width * D_STATE * 4
            + rows * width * 4 + rows * LANES * 12 + 16 * nct * width * 4 + 24 * CHUNK * width * 4)
    return pl.pallas_call(
        functools.partial(_ssd_kernel, nb=nb, nc=nc, hpg=hpg, has_h0=has_h0, want_state=want_state),
        grid=(n_seq // nb, N_GROUPS),
        in_specs=in_specs,
        out_specs=out_specs,
        out_shape=out_shape,
        scratch_shapes=scratch,
        compiler_params=_cparams(("arbitrary", "arbitrary"), need + 6 * 2**20),
        name="ssd_scan",
    )(*args)


def _final_norm_kernel(x_ref, w_ref, o_ref):
    x = x_ref[...]
    o_ref[...] = x * lax.rsqrt(jnp.mean(x * x, axis=-1, keepdims=True) + EPS) * w_ref[...]


def _final_norm(h, w):
    rows, d = h.shape
    tm = _TILES["norm_rows"]
    need = 2 * (2 * tm * d * 4) + 4 * tm * d * 4
    return pl.pallas_call(
        _final_norm_kernel,
        grid=(rows // tm,),
        in_specs=[pl.BlockSpec((tm, d), lambda i: (i, 0)), pl.BlockSpec((1, d), lambda i: (0, 0))],
        out_specs=pl.BlockSpec((tm, d), lambda i: (i, 0)),
        out_shape=jax.ShapeDtypeStruct((rows, d), F32),
        compiler_params=_cparams(("arbitrary",), need),
        name="final_norm",
    )(h, w.reshape(1, d))


def kernel(x_prompt, x_sample, state_ssd, c, c_ctx, ada_w, ada_b, norm_w, pool_in_w, pool_grp_w, pool_grp_b, pool_scale, pool_out_w, ssd_in_w, ssd_conv_w, ssd_conv_b, ssd_dt_bias, ssd_A_log, ssd_D, ssd_norm_w, ssd_out_w, final_norm_w):
    batch, seq, d = x_prompt.shape
    dec_batch, dec_seq, _ = x_sample.shape
    n_ctx, n_lat = batch * seq, dec_batch * dec_seq
    depth = ada_w.shape[0]
    n_heads = ssd_A_log.shape[-1]
    d_inner = n_heads * HEAD_DIM
    e_pool = pool_out_w.shape[1]

    c_rows = jnp.concatenate([c_ctx[None, :], c, jnp.zeros((8 - 1 - dec_batch, d), F32)], axis=0)
    mod = _ada_mod(c_rows, ada_w, ada_b).reshape(depth, 8, 3, d)
    nseg = 1 + dec_batch
    part = lambda layer, which: mod[layer, :nseg, which][:, None, :]

    h_ctx = x_prompt.reshape(n_ctx, d)
    h_lat = x_sample.reshape(n_lat, d)
    new_states = []
    for layer in range(depth):
        j = layer // 2
        shift, scale, gate = (part(layer, k) for k in range(3))
        u = _prologue(h_ctx, h_lat, norm_w[layer].reshape(1, d), shift, scale, dec_seq)

        if layer % 2 == 0:
            w_in = pool_in_w[j].astype(BF16)
            w_fold = _matmul_grouped(w_in, pool_grp_w[j].astype(BF16), BF16)
            y = _matmul(u, w_fold, BF16)
            z = _matmul(u, w_in, BF16, b_col0=e_pool, n=e_pool)
            v_ctx = _poolmix(y, z, _pool_membership(seq, None), pool_grp_b[j], pool_scale[j],
                             0, n_ctx, seq, _TILES["pool_ctx"], None)
            v_lat = _poolmix(y, z, _pool_membership(dec_seq, GRID_W), pool_grp_b[j], pool_scale[j],
                             n_ctx, n_lat, dec_seq, _TILES["pool_lat"], GRID_W)
            w_out = pool_out_w[j].astype(BF16)
            ssq_ctx = ssq_lat = None
        else:
            w_in = ssd_in_w[j].astype(BF16)
            n_main = 2 * d_inner + 2 * N_GROUPS * D_STATE
            proj = _matmul(u, w_in, BF16, n=n_main)
            dt_raw = _matmul(u, w_in, F32, b_col0=n_main, n=2 * n_heads, bn=2 * n_heads)
            n_conv = n_main - d_inner
            d_exp = jnp.repeat(ssd_D[j], HEAD_DIM).reshape(1, d_inner)
            conv = lambda row0, n_rows, sq, rows: _conv_silu(proj, d_inner, n_conv, ssd_conv_w[j], ssd_conv_b[j], row0, n_rows, sq, rows)
            ssd = functools.partial(_ssd, proj=proj, dt_raw=dt_raw, dt_bias=ssd_dt_bias[j], a_log=ssd_A_log[j],
                                    d_skip_exp=d_exp, gnorm_w=ssd_norm_w[j], n_heads=n_heads)
            v_ctx, ssq_ctx, st = ssd(conv(0, n_ctx, seq, 4 * seq), h0=None, row0=0, n_seq=batch, seq=seq, want_state=True,
                                     nb=_TILES["ssd_seqs_ctx"])
            h0 = state_ssd[:, j].reshape(dec_batch, 2, d_inner, D_STATE)
            v_lat, ssq_lat = ssd(conv(n_ctx, n_lat, dec_seq, dec_seq), h0=h0, row0=n_ctx, n_seq=dec_batch, seq=dec_seq,
                                 want_state=False, nb=_TILES["ssd_seqs_lat"])
            new_states.append(st.reshape(batch, 2, n_heads, HEAD_DIM, D_STATE))
            w_out = ssd_out_w[j].astype(BF16)

        h_ctx = _matmul_residual(v_ctx, w_out, h_ctx, gate, 0, n_ctx, ssq_ctx)
        h_lat = _matmul_residual(v_lat, w_out, h_lat, gate, 1, dec_seq, ssq_lat)

    y_prompt = _final_norm(h_ctx, final_norm_w).reshape(batch, seq, d)
    y_sample = _final_norm(h_lat, final_norm_w).reshape(dec_batch, dec_seq, d)
    return (y_prompt, y_sample, jnp.stack(new_states, axis=1))
```

```python
import functools

import jax
import jax.numpy as jnp
from jax import lax
from jax.experimental import pallas as pl
from jax.experimental.pallas import tpu as pltpu

F32 = jnp.float32
BF16 = jnp.bfloat16

EPS = 1e-6
POOL_WINDOWS = (2, 4, 8, 16)
GRID_W = 64
HEAD_DIM = 64
D_STATE = 128
N_GROUPS = 8
CONV_W = 7
CHUNK = 128
LANES = 128

V7X_VMEM_BYTES = 64 * 2**20
_VMEM_HEADROOM = 6 * 2**20

_TILES = dict(
    ada_bn=512,
    norm_rows=256,
    proj=(1024, 1024),
    fold=(512, 1024),
    out=(1024, 256, 8192),
    pool_ctx=(4, 1024),
    pool_lat=(1, 512),
    conv_cols=512,
    ssd_seqs_ctx=2,
    ssd_seqs_lat=1,
)


def _cparams(semantics, vmem_need):
    limit = min(max(int(vmem_need), 16 * 2**20), V7X_VMEM_BYTES - _VMEM_HEADROOM)
    return pltpu.CompilerParams(dimension_semantics=semantics, vmem_limit_bytes=limit)


def _silu(t):
    return t * jax.nn.sigmoid(t)


def _seg(tile, rows_per_tile, n_ctx, dec_seq):
    start = tile * rows_per_tile
    return jnp.where(start < n_ctx, 0, 1 + jnp.maximum(start - n_ctx, 0) // dec_seq)


def _ada_kernel(c_ref, w_ref, b_ref, o_ref):
    s = _silu(c_ref[...]).astype(BF16)
    o_ref[0] = jnp.dot(s, w_ref[0].astype(BF16), preferred_element_type=F32) + b_ref[0]


def _ada_mod(c_rows, ada_w, ada_b):
    depth, d, n3 = ada_w.shape
    rows = c_rows.shape[0]
    bn = min(_TILES["ada_bn"], n3)
    need = 2 * (d * bn * 4) + d * bn * 2 + 4 * rows * d * 4
    return pl.pallas_call(
        _ada_kernel,
        grid=(depth, n3 // bn),
        in_specs=[
            pl.BlockSpec((rows, d), lambda l, j: (0, 0)),
            pl.BlockSpec((1, d, bn), lambda l, j: (l, 0, j)),
            pl.BlockSpec((1, 1, bn), lambda l, j: (l, 0, j)),
        ],
        out_specs=pl.BlockSpec((1, rows, bn), lambda l, j: (l, 0, j)),
        out_shape=jax.ShapeDtypeStruct((depth, rows, n3), F32),
        compiler_params=_cparams(("arbitrary", "arbitrary"), need + 8 * 2**20),
        name="ada_mod",
    )(c_rows, ada_w, ada_b.reshape(depth, 1, n3))


def _norm_mod(x, w, shift, scale):
    xn = x * lax.rsqrt(jnp.mean(x * x, axis=-1, keepdims=True) + EPS) * w
    return xn * (1.0 + scale) + shift


def _prologue_kernel(xc_ref, xl_ref, w_ref, sh_ref, sc_ref, u_ref, *, ctx_tiles):
    x = jnp.where(pl.program_id(0) < ctx_tiles, xc_ref[...], xl_ref[...])
    u_ref[...] = _norm_mod(x, w_ref[...], sh_ref[0], sc_ref[0]).astype(BF16)


def _prologue(h_ctx, h_lat, w, shift, scale, dec_seq):
    n_ctx, d = h_ctx.shape
    t = n_ctx + h_lat.shape[0]
    tm = _TILES["norm_rows"]
    ctx_tiles = n_ctx // tm
    seg = lambda i: (_seg(i, tm, n_ctx, dec_seq), 0, 0)
    need = 2 * (2 * tm * d * 4 + tm * d * 2) + 6 * tm * d * 4
    return pl.pallas_call(
        functools.partial(_prologue_kernel, ctx_tiles=ctx_tiles),
        grid=(t // tm,),
        in_specs=[
            pl.BlockSpec((tm, d), lambda i: (jnp.minimum(i, ctx_tiles - 1), 0)),
            pl.BlockSpec((tm, d), lambda i: (jnp.maximum(i - ctx_tiles, 0), 0)),
            pl.BlockSpec((1, d), lambda i: (0, 0)),
            pl.BlockSpec((1, 1, d), seg),
            pl.BlockSpec((1, 1, d), seg),
        ],
        out_specs=pl.BlockSpec((tm, d), lambda i: (i, 0)),
        out_shape=jax.ShapeDtypeStruct((t, d), BF16),
        compiler_params=_cparams(("arbitrary",), need),
        name="prologue",
    )(h_ctx, h_lat, w, shift, scale)


def _rider_specs(rider, n_steps, step_of):
    src, col_block, n_cols = rider
    r = src.shape[0]
    rb = 16
    while r % rb or r // rb > n_steps:
        rb += 16
    last = r // rb - 1
    return (pl.BlockSpec((rb, n_cols), lambda *ids: (jnp.minimum(step_of(*ids), last), col_block)),
            pl.BlockSpec((rb, n_cols), lambda *ids: (jnp.minimum(step_of(*ids), last), 0)),
            jax.ShapeDtypeStruct((r, n_cols), BF16), 2 * rb * n_cols * (4 + 2))


def _mm_kernel(*refs, has_rider, b_index):
    a_ref, b_ref = refs[:2]
    o_ref = refs[3 if has_rider else 2]
    b = b_ref[...] if b_index is None else b_ref[b_index]
    o_ref[...] = jnp.dot(a_ref[...].astype(BF16), b.astype(BF16), preferred_element_type=F32).astype(o_ref.dtype)
    if has_rider:
        refs[4][...] = refs[2][...].astype(BF16)


def _matmul(a, b, out_dtype, b_col0=0, n=None, bn=None, rider=None):
    m, k = a.shape
    n = b.shape[1] if n is None else n
    bm = min(_TILES["proj"][0], m)
    bn = min(_TILES["proj"][1] if bn is None else bn, n)
    j0 = b_col0 // bn
    nj = n // bn
    osz = jnp.dtype(out_dtype).itemsize
    need = 2 * (bm * k * 2 + k * bn * 2 + bm * bn * osz) + bm * bn * 4
    in_specs = [pl.BlockSpec((bm, k), lambda i, j: (i, 0)), pl.BlockSpec((k, bn), lambda i, j: (0, j + j0))]
    out_specs = [pl.BlockSpec((bm, bn), lambda i, j: (i, j))]
    out_shape = [jax.ShapeDtypeStruct((m, n), out_dtype)]
    args = [a, b]
    if rider is not None:
        r_in, r_out, r_shape, r_bytes = _rider_specs(rider, (m // bm) * nj, lambda i, j: i * nj + j)
        in_specs.append(r_in), out_specs.append(r_out), out_shape.append(r_shape), args.append(rider[0])
        need += r_bytes
    res = pl.pallas_call(
        functools.partial(_mm_kernel, has_rider=rider is not None, b_index=None),
        grid=(m // bm, nj),
        in_specs=in_specs,
        out_specs=out_specs,
        out_shape=out_shape,
        compiler_params=_cparams(("arbitrary", "arbitrary"), need + 4 * 2**20),
        name="matmul",
    )(*args)
    return res[0], (res[1] if rider is not None else None)


def _matmul_grouped(a, b, out_dtype, rider=None):
    m = a.shape[0]
    groups, k, n = b.shape
    bm, bn = min(_TILES["fold"][0], m), min(_TILES["fold"][1], n)
    ni, nj = m // bm, n // bn
    osz = jnp.dtype(out_dtype).itemsize
    need = (2 * (bm * k * a.dtype.itemsize + k * bn * b.dtype.itemsize + bm * bn * osz)
            + bm * bn * 4 + (bm * k + k * bn) * 2)
    in_specs = [pl.BlockSpec((bm, k), lambda g, i, j: (i, g)), pl.BlockSpec((1, k, bn), lambda g, i, j: (g, 0, j))]
    out_specs = [pl.BlockSpec((bm, bn), lambda g, i, j: (i, g * nj + j))]
    out_shape = [jax.ShapeDtypeStruct((m, groups * n), out_dtype)]
    args = [a, b]
    if rider is not None:
        r_in, r_out, r_shape, r_bytes = _rider_specs(rider, groups * ni * nj, lambda g, i, j: (g * ni + i) * nj + j)
        in_specs.append(r_in), out_specs.append(r_out), out_shape.append(r_shape), args.append(rider[0])
        need += r_bytes
    res = pl.pallas_call(
        functools.partial(_mm_kernel, has_rider=rider is not None, b_index=0),
        grid=(groups, ni, nj),
        in_specs=in_specs,
        out_specs=out_specs,
        out_shape=out_shape,
        compiler_params=_cparams(("arbitrary", "arbitrary", "arbitrary"), need + 4 * 2**20),
        name="matmul_grouped",
    )(*args)
    return res[0], (res[1] if rider is not None else None)


def _mm_res_kernel(*refs, nk, norm_dim, has_rider):
    it = iter(refs)
    a_ref, b_ref, h_ref, g_ref = (next(it) for _ in range(4))
    ssq_ref = next(it) if norm_dim else None
    rsrc_ref = next(it) if has_rider else None
    o_ref = next(it)
    if has_rider:
        next(it)[...] = rsrc_ref[...].astype(BF16)

    def finish(acc):
        if norm_dim:
            ms = jnp.sum(ssq_ref[...], axis=1, keepdims=True) * (1.0 / norm_dim)
            acc = acc * lax.rsqrt(ms + EPS)
        o_ref[...] = h_ref[...] + g_ref[0] * acc

    if nk == 1:
        finish(jnp.dot(a_ref[...], b_ref[...], preferred_element_type=F32))
        return
    acc_ref = refs[-1]
    k = pl.program_id(2)

    @pl.when(k == 0)
    def _():
        acc_ref[...] = jnp.zeros_like(acc_ref)

    acc_ref[...] += jnp.dot(a_ref[...], b_ref[...], preferred_element_type=F32)

    @pl.when(k == nk - 1)
    def _():
        finish(acc_ref[...])


def _matmul_residual(a, b, h, gate, seg0, seg_rows, ssq=None, rider=None):
    m, kdim = a.shape
    n = b.shape[1]
    bm, bn, bk = (min(t, s) for t, s in zip(_TILES["out"], (m, n, kdim)))
    nj, nk = n // bn, kdim // bk
    need = 2 * (bm * bk * 2 + bk * bn * 2 + 2 * bm * bn * 4) + 2 * bm * bn * 4
    in_specs = [
        pl.BlockSpec((bm, bk), lambda i, j, k: (i, k)),
        pl.BlockSpec((bk, bn), lambda i, j, k: (k, j)),
        pl.BlockSpec((bm, bn), lambda i, j, k: (i, j)),
        pl.BlockSpec((1, 1, bn), lambda i, j, k: (seg0 + (i * bm) // seg_rows, 0, j)),
    ]
    args = [a, b, h, gate]
    if ssq is not None:
        in_specs.append(pl.BlockSpec((bm, ssq.shape[1]), lambda i, j, k: (i, 0)))
        args.append(ssq)
        need += 2 * bm * ssq.shape[1] * 4
    out_specs = [pl.BlockSpec((bm, bn), lambda i, j, k: (i, j))]
    out_shape = [jax.ShapeDtypeStruct((m, n), F32)]
    if rider is not None:
        r_in, r_out, r_shape, r_bytes = _rider_specs(rider, (m // bm) * nj * nk, lambda i, j, k: (i * nj + j) * nk + k)
        in_specs.append(r_in), out_specs.append(r_out), out_shape.append(r_shape), args.append(rider[0])
        need += r_bytes
    res = pl.pallas_call(
        functools.partial(_mm_res_kernel, nk=nk, norm_dim=kdim if ssq is not None else 0, has_rider=rider is not None),
        grid=(m // bm, nj, nk),
        in_specs=in_specs,
        out_specs=out_specs,
        out_shape=out_shape,
        scratch_shapes=[pltpu.VMEM((bm, bn), F32)] if nk > 1 else [],
        compiler_params=_cparams(("arbitrary", "arbitrary", "arbitrary"), need + 4 * 2**20),
        name="matmul_residual",
    )(*args)
    return res[0], (res[1] if rider is not None else None)


def _window_members(pos_t, pos_s, w):
    lo = pos_t - w // 2
    return (pos_s >= lo) & (pos_s < lo + w)


def _pool_membership(seq, grid_w):
    t = jnp.arange(seq, dtype=jnp.int32)[:, None]
    s = jnp.arange(seq, dtype=jnp.int32)[None, :]
    mats = []
    for w in POOL_WINDOWS:
        if grid_w is None:
            m = _window_members(t, s, w)
        else:
            m = _window_members(t // grid_w, s // grid_w, w) & _window_members(t % grid_w, s % grid_w, w)
        mats.append(m)
    return jnp.stack(mats).astype(BF16)


def _poolmix_kernel(y_ref, z_ref, p_ref, b_ref, sc_ref, o_ref, *, seq, nseq, rows_step, reaches, cols_per_group):
    bias = b_ref[...]
    scale = sc_ref[...]

    def mix(reach):
        for s in range(nseq):
            for r0 in range(0, seq, rows_step):
                k0, k1 = max(0, r0 - reach), min(seq, r0 + rows_step + reach)
                member = p_ref[0, pl.ds(r0, rows_step), pl.ds(k0, k1 - k0)]
                cnt = jnp.sum(member.astype(F32), axis=1, keepdims=True)
                rows = pl.ds(s * seq + r0, rows_step)
                tot = jnp.dot(member, y_ref[pl.ds(s * seq + k0, k1 - k0), :], preferred_element_type=F32)
                mixed = tot / cnt - y_ref[rows, :].astype(F32) + bias
                z = z_ref[rows, :].astype(F32)
                o_ref[rows, :] = (mixed * scale * _silu(z)).astype(BF16)

    if seq <= rows_step:
        mix(seq)
    else:
        group = pl.program_id(0) // cols_per_group
        for gi, reach in enumerate(reaches):
            pl.when(group == gi)(functools.partial(mix, reach))


def _poolmix(y, z, member, grp_b, scale, row0, n_rows, seq, tile, grid_w):
    e = y.shape[1]
    gw = e // len(POOL_WINDOWS)
    nseq, cb = tile[0], min(tile[1], gw)
    rows = seq * nseq
    r0 = row0 // rows
    rows_step = min(seq, 512)
    reaches = tuple(-(-(w // 2) * (grid_w or 1) // LANES) * LANES for w in POOL_WINDOWS)
    need = 2 * (3 * rows * cb * 2 + seq * seq * 2) + 6 * rows_step * cb * 4 + rows_step * seq * 4
    return pl.pallas_call(
        functools.partial(_poolmix_kernel, seq=seq, nseq=nseq, rows_step=rows_step, reaches=reaches, cols_per_group=gw // cb),
        grid=(e // cb, n_rows // rows),
        in_specs=[
            pl.BlockSpec((rows, cb), lambda c, i: (i + r0, c)),
            pl.BlockSpec((rows, cb), lambda c, i: (i + r0, c)),
            pl.BlockSpec((1, seq, seq), lambda c, i: (c * cb // gw, 0, 0)),
            pl.BlockSpec((1, cb), lambda c, i: (0, c)),
            pl.BlockSpec((1, cb), lambda c, i: (0, c)),
        ],
        out_specs=pl.BlockSpec((rows, cb), lambda c, i: (i, c)),
        out_shape=jax.ShapeDtypeStruct((n_rows, e), BF16),
        compiler_params=_cparams(("arbitrary", "arbitrary"), need + 8 * 2**20),
        name="poolmix",
    )(y, z, member, grp_b.reshape(1, e), scale.reshape(1, e))


_CONV_PAD = 16
_CONV_ROWS = 128


def _conv_kernel(x_ref, w_ref, b_ref, o_ref, xpad_ref, *, seq):
    rows, cb = x_ref.shape
    half = CONV_W // 2
    pitch = seq + _CONV_PAD
    win = _CONV_ROWS + 2 * _CONV_PAD
    taps = [k for k in range(CONV_W) if k != half]
    out_row = lax.broadcasted_iota(jnp.int32, (len(taps) * _CONV_ROWS, win), 0)
    src_row = lax.broadcasted_iota(jnp.int32, (len(taps) * _CONV_ROWS, win), 1)
    offset = sum(jnp.where(out_row // _CONV_ROWS == i, k - half, 0) for i, k in enumerate(taps))
    shift_all = (src_row == out_row % _CONV_ROWS + _CONV_PAD + offset).astype(BF16)

    for s in range(rows // seq + 1):
        xpad_ref[pl.ds(s * pitch, _CONV_PAD), :] = jnp.zeros((_CONV_PAD, cb), BF16)
    for s in range(rows // seq):
        xpad_ref[pl.ds(s * pitch + _CONV_PAD, seq), :] = x_ref[pl.ds(s * seq, seq), :]
    bias = b_ref[...]
    for s in range(rows // seq):
        for j in range(seq // _CONV_ROWS):
            base = s * pitch + _CONV_PAD + j * _CONV_ROWS
            shifted = jnp.dot(shift_all, xpad_ref[pl.ds(base - _CONV_PAD, win), :], preferred_element_type=F32)
            out_rows = pl.ds(s * seq + j * _CONV_ROWS, _CONV_ROWS)
            acc = bias + w_ref[pl.ds(half, 1), :] * x_ref[out_rows, :].astype(F32)
            for i, k in enumerate(taps):
                acc = acc + w_ref[pl.ds(k, 1), :] * shifted[i * _CONV_ROWS:(i + 1) * _CONV_ROWS]
            o_ref[out_rows, :] = _silu(acc).astype(BF16)


def _conv_silu(proj, col0, n_cols, conv_w, conv_b, row0, n_rows, seq, rows):
    assert rows % seq == 0 and seq % _CONV_ROWS == 0 and CONV_W // 2 <= _CONV_PAD
    cb = _TILES["conv_cols"]
    r0 = row0 // rows
    c0 = col0 // cb
    padded = (rows // seq) * (seq + _CONV_PAD) + _CONV_PAD
    need = 2 * (2 * rows * cb * 2) + padded * cb * 2 + 16 * _CONV_ROWS * cb * 4
    return pl.pallas_call(
        functools.partial(_conv_kernel, seq=seq),
        grid=(n_rows // rows, n_cols // cb),
        in_specs=[
            pl.BlockSpec((rows, cb), lambda i, c: (i + r0, c + c0)),
            pl.BlockSpec((CONV_W, cb), lambda i, c: (0, c)),
            pl.BlockSpec((1, cb), lambda i, c: (0, c)),
        ],
        out_specs=pl.BlockSpec((rows, cb), lambda i, c: (i, c)),
        out_shape=jax.ShapeDtypeStruct((n_rows, n_cols), BF16),
        scratch_shapes=[pltpu.VMEM((padded, cb), BF16)],
        compiler_params=_cparams(("arbitrary", "arbitrary"), need + 4 * 2**20),
        name="conv_silu",
    )(proj, conv_w, conv_b.reshape(1, n_cols))


_LOG2E = 1.4426950408889634


def _split3(v):
    hi = v.astype(BF16)
    r1 = v - hi.astype(F32)
    mid = r1.astype(BF16)
    lo = (r1 - mid.astype(F32)).astype(BF16)
    return hi, mid, lo


def _pack3(v, stride):
    hi, mid, lo = _split3(v)
    packed = hi.astype(F32) + pltpu.roll(mid.astype(F32), stride, axis=1) + pltpu.roll(lo.astype(F32), 2 * stride, axis=1)
    return packed.astype(BF16)


def _ssd_kernel(*refs, nb, nc, hpg, has_h0, want_state):
    it = iter(refs)
    x_ref, b_ref, c_ref, dt_ref, dtb_ref, alog_ref, dskip_ref, z_ref, gw_ref = (next(it) for _ in range(9))
    h0_ref = next(it) if has_h0 else None
    yg_ref, ssq_ref = next(it), next(it)
    st_ref = next(it) if want_state else None
    h_scr, y_scr, cum_scr, w3_scr, rowp_scr, bt_scr, s_scr, cd_scr = (next(it) for _ in range(8))

    q = CHUNK
    nct = nb * nc
    width = hpg * HEAD_DIM
    stride3 = 2 * hpg
    assert q == D_STATE == LANES and hpg % 2 == 0 and 2 * HEAD_DIM == LANES and 3 * stride3 <= LANES
    g = pl.program_id(1)
    shift0 = (LANES - hpg * g) % LANES
    shifts = (shift0, (shift0 + hpg) % LANES)

    lane = lax.broadcasted_iota(jnp.int32, (q, LANES), 1)
    lane_all = lax.broadcasted_iota(jnp.int32, (nct * q, LANES), 1)
    lane1 = lax.broadcasted_iota(jnp.int32, (1, LANES), 1)
    row_i = lax.broadcasted_iota(jnp.int32, (q, q), 0)
    col_i = lax.broadcasted_iota(jnp.int32, (q, q), 1)
    masks = (row_i >= col_i, row_i <= col_i)
    tri = tuple(m.astype(BF16) for m in masks)
    ones = jnp.ones((q, q), BF16)
    e_row = lax.broadcasted_iota(jnp.int32, (LANES, width), 0)
    e_head = lax.broadcasted_iota(jnp.int32, (LANES, width), 1) // HEAD_DIM
    expand = tuple(((e_row % stride3 == d * hpg + e_head) & (e_row < 3 * stride3)).astype(BF16) for d in range(2))
    first_half = lane < HEAD_DIM

    def pick(v0, v1, ln):
        return jnp.where(ln < hpg, v0, jnp.where(ln < 2 * hpg, v1, 0.0))

    def unpack3(p):
        s = p + pltpu.roll(p, LANES - stride3, axis=1) + pltpu.roll(p, LANES - 2 * stride3, axis=1)
        return jnp.where(lane_all < stride3, s, 0.0)

    for sl in range(nb):
        for d in range(2):
            if has_h0:
                h_scr[2 * sl + d] = jnp.transpose(h0_ref[sl, d])
            else:
                h_scr[2 * sl + d] = jnp.zeros((D_STATE, width), F32)

    neg_a = pick(*(-jnp.exp(pltpu.roll(alog_ref[pl.ds(d, 1), :], shifts[d], axis=1)) for d in range(2)), lane1)
    dt_bias = pick(*(pltpu.roll(dtb_ref[pl.ds(d, 1), :], shifts[d], axis=1) for d in range(2)), lane1)
    dt_raw = pick(*(pltpu.roll(dt_ref[:, pl.ds(d * LANES, LANES)], shifts[d], axis=1) for d in range(2)), lane_all)
    dt = jnp.where(lane_all < 2 * hpg, jax.nn.softplus(dt_raw + dt_bias), 0.0)
    a3 = _pack3(dt * neg_a, stride3)
    chunks = [a3[k * q:(k + 1) * q] for k in range(nct)]
    pfx, sfx, tot = (unpack3(jnp.concatenate([jnp.dot(m, ck, preferred_element_type=F32) for ck in chunks], axis=0))
                     for m in (tri[0], tri[1], ones))
    cum = jnp.where(lane_all < hpg, pfx, sfx)
    cum_scr[...] = cum * _LOG2E
    w3_scr[...] = _pack3(dt * jnp.exp(tot - cum), stride3)
    rowp = (cum - jnp.log(dt)) * _LOG2E
    decay8 = jnp.concatenate([jnp.exp(tot[k * q:k * q + 8]) for k in range(nct)], axis=0)
    lane8 = lax.broadcasted_iota(jnp.int32, (8 * nct, LANES), 1)
    cd3 = _pack3(jnp.where(lane8 < stride3, decay8, 0.0), stride3)
    for d in range(2):
        cd_scr[d] = jnp.dot(cd3, expand[d], preferred_element_type=F32)
    for k in range(nct):
        rows = pl.ds(k * q, q)
        rowp_scr[pl.ds(k * stride3, stride3), :] = jnp.transpose(rowp[k * q:(k + 1) * q])[:stride3, :]
        bc = b_ref[rows, :]
        bt_scr[rows, :] = jnp.transpose(bc.astype(F32)).astype(BF16)
        s_scr[rows, :] = lax.dot_general(c_ref[rows, :], bc, (((1,), (1,)), ((), ())), preferred_element_type=F32).astype(BF16)
    y_scr[...] = x_ref[...].astype(F32) * dskip_ref[...]

    def scan(i, carry):
        for sl in range(nb):
            for d in range(2):
                k = sl * nc + (i if d == 0 else nc - 1 - i)
                rows = pl.ds(pl.multiple_of(k * q, q), q)
                cum_c = cum_scr[rows, :]
                scores = s_scr[rows, :]
                cc = c_ref[rows, :]
                h_in = h_scr[2 * sl + d]
                h_bf = h_in.astype(BF16)
                for p in range(hpg // 2):
                    lanes = pl.ds(p * LANES, LANES)
                    rhs = jnp.concatenate([x_ref[rows, lanes], h_bf[:, p * LANES:(p + 1) * LANES]], axis=0)
                    halves = []
                    for r in (2 * p, 2 * p + 1):
                        col = jnp.broadcast_to(cum_c[:, d * hpg + r:d * hpg + r + 1], (q, q))
                        row = rowp_scr[pl.ds(k * stride3 + d * hpg + r, 1), :]
                        decay = jnp.exp2(jnp.where(masks[d], col - row, -jnp.inf)).astype(BF16)
                        lhs = jnp.concatenate([scores * decay, cc * jnp.exp2(col).astype(BF16)], axis=1)
                        halves.append(jnp.dot(lhs, rhs, preferred_element_type=F32))
                    y_scr[rows, lanes] += jnp.where(first_half, halves[0], halves[1])
                w_exp = jnp.dot(w3_scr[rows, :], expand[d], preferred_element_type=F32)
                xd = (x_ref[rows, :].astype(F32) * w_exp).astype(BF16)
                states = jnp.dot(bt_scr[rows, :], xd, preferred_element_type=F32)
                h_scr[2 * sl + d] = h_in * cd_scr[d, pl.ds(pl.multiple_of(k * 8, 8), 1), :] + states
        return carry

    lax.fori_loop(0, nc, scan, 0)

    def gate(k, carry):
        rows = pl.ds(pl.multiple_of(k * q, q), q)
        t = y_scr[rows, :] * _silu(z_ref[rows, :].astype(F32))
        yg_ref[rows, :] = (t * gw_ref[...]).astype(BF16)
        t2 = t * t
        ssq_ref[rows, :] = sum(t2[:, j * LANES:(j + 1) * LANES] for j in range(width // LANES))
        return carry

    lax.fori_loop(0, nct, gate, 0)

    if want_state:
        for sl in range(nb):
            for d in range(2):
                st_ref[sl, d] = jnp.transpose(h_scr[2 * sl + d])


def _ssd(xbc, proj, dt_raw, dt_bias, a_log, d_skip_exp, gnorm_w, h0, row0, n_seq, seq, n_heads, want_state, nb):
    d_inner = n_heads * HEAD_DIM
    hpg = n_heads // N_GROUPS
    width = hpg * HEAD_DIM
    nc = seq // CHUNK
    rows = nb * seq
    r0 = row0 // rows
    b0 = d_inner // D_STATE
    has_h0 = h0 is not None
    in_specs = [
        pl.BlockSpec((rows, width), lambda s, g: (s, g)),
        pl.BlockSpec((rows, D_STATE), lambda s, g: (s, b0 + g)),
        pl.BlockSpec((rows, D_STATE), lambda s, g: (s, b0 + N_GROUPS + g)),
        pl.BlockSpec((rows, 2 * n_heads), lambda s, g: (s + r0, 0)),
        pl.BlockSpec((2, n_heads), lambda s, g: (0, 0)),
        pl.BlockSpec((2, n_heads), lambda s, g: (0, 0)),
        pl.BlockSpec((1, width), lambda s, g: (0, g)),
        pl.BlockSpec((rows, width), lambda s, g: (s + r0, g)),
        pl.BlockSpec((1, width), lambda s, g: (0, g)),
    ]
    args = [xbc, xbc, xbc, dt_raw, dt_bias, a_log, d_skip_exp, proj, gnorm_w.reshape(1, d_inner)]
    if has_h0:
        in_specs.append(pl.BlockSpec((nb, 2, width, D_STATE), lambda s, g: (s, 0, g, 0)))
        args.append(h0)
    out_specs = [pl.BlockSpec((rows, width), lambda s, g: (s, g)), pl.BlockSpec((rows, LANES), lambda s, g: (s, g))]
    out_shape = [jax.ShapeDtypeStruct((n_seq * seq, d_inner), BF16), jax.ShapeDtypeStruct((n_seq * seq, N_GROUPS * LANES), F32)]
    if want_state:
        out_specs.append(pl.BlockSpec((nb, 2, width, D_STATE), lambda s, g: (s, 0, g, 0)))
        out_shape.append(jax.ShapeDtypeStruct((n_seq, 2, d_inner, D_STATE), F32))
    nct = nb * nc
    scratch = [
        pltpu.VMEM((2 * nb, D_STATE, width), F32),
        pltpu.VMEM((rows, width), F32),
        pltpu.VMEM((rows, LANES), F32),
        pltpu.VMEM((rows, LANES), BF16),
        pltpu.VMEM((nct * 2 * hpg, CHUNK), F32),
        pltpu.VMEM((rows, CHUNK), BF16),
        pltpu.VMEM((rows, CHUNK), BF16),
        pltpu.VMEM((2, 8 * nct, width), F32),
    ]
    need = (2 * (2 * rows * width * 2 + 2 * rows * D_STATE * 2 + rows * 2 * n_heads * 4 + rows * width * 2 + rows * LANES * 4)
            + ((4 if has_h0 else 0) + (4 if want_state else 0) + 2) * nb * width * D_STATE * 4
            + rows * width * 4 + rows * LANES * 12 + 16 * nct * width * 4 + 24 * CHUNK * width * 4)
    return pl.pallas_call(
        functools.partial(_ssd_kernel, nb=nb, nc=nc, hpg=hpg, has_h0=has_h0, want_state=want_state),
        grid=(n_seq // nb, N_GROUPS),
        in_specs=in_specs,
        out_specs=out_specs,
        out_shape=out_shape,
        scratch_shapes=scratch,
        compiler_params=_cparams(("arbitrary", "arbitrary"), need + 6 * 2**20),
        name="ssd_scan",
    )(*args)


def _final_norm_kernel(x_ref, w_ref, o_ref):
    x = x_ref[...]
    o_ref[...] = x * lax.rsqrt(jnp.mean(x * x, axis=-1, keepdims=True) + EPS) * w_ref[...]


def _final_norm(h, w):
    rows, d = h.shape
    tm = _TILES["norm_rows"]
    need = 2 * (2 * tm * d * 4) + 4 * tm * d * 4
    return pl.pallas_call(
        _final_norm_kernel,
        grid=(rows // tm,),
        in_specs=[pl.BlockSpec((tm, d), lambda i: (i, 0)), pl.BlockSpec((1, d), lambda i: (0, 0))],
        out_specs=pl.BlockSpec((tm, d), lambda i: (i, 0)),
        out_shape=jax.ShapeDtypeStruct((rows, d), F32),
        compiler_params=_cparams(("arbitrary",), need),
        name="final_norm",
    )(h, w.reshape(1, d))


def kernel(x_prompt, x_sample, state_ssd, c, c_ctx, ada_w, ada_b, norm_w, pool_in_w, pool_grp_w, pool_grp_b, pool_scale, pool_out_w, ssd_in_w, ssd_conv_w, ssd_conv_b, ssd_dt_bias, ssd_A_log, ssd_D, ssd_norm_w, ssd_out_w, final_norm_w):
    batch, seq, d = x_prompt.shape
    dec_batch, dec_seq, _ = x_sample.shape
    n_ctx, n_lat = batch * seq, dec_batch * dec_seq
    depth = ada_w.shape[0]
    n_heads = ssd_A_log.shape[-1]
    d_inner = n_heads * HEAD_DIM
    e_pool = pool_out_w.shape[1]

    c_rows = jnp.concatenate([c_ctx[None, :], c, jnp.zeros((8 - 1 - dec_batch, d), F32)], axis=0)
    mod = _ada_mod(c_rows, ada_w, ada_b).reshape(depth, 8, 3, d)
    nseg = 1 + dec_batch
    part = lambda layer, which: mod[layer, :nseg, which][:, None, :]

    h_ctx = x_prompt.reshape(n_ctx, d)
    h_lat = x_sample.reshape(n_lat, d)
    new_states = []
    carried = {}
    for layer in range(depth):
        j = layer // 2
        shift, scale, gate = (part(layer, k) for k in range(3))
        u = _prologue(h_ctx, h_lat, norm_w[layer].reshape(1, d), shift, scale, dec_seq)
        out_rider = None

        if layer % 2 == 0:
            nxt = layer + 1 < depth
            w_in = pool_in_w[j]
            w_fold, w_z = _matmul_grouped(w_in, pool_grp_w[j], BF16, rider=(w_in, 1, e_pool))
            y, w_out = _matmul(u, w_fold, BF16, rider=(pool_out_w[j], 0, d))
            z, carried["ssd_out"] = _matmul(u, w_z, BF16, rider=(ssd_out_w[j], 0, d) if nxt else None)
            v_ctx = _poolmix(y, z, _pool_membership(seq, None), pool_grp_b[j], pool_scale[j],
                             0, n_ctx, seq, _TILES["pool_ctx"], None)
            v_lat = _poolmix(y, z, _pool_membership(dec_seq, GRID_W), pool_grp_b[j], pool_scale[j],
                             n_ctx, n_lat, dec_seq, _TILES["pool_lat"], GRID_W)
            out_rider = (ssd_in_w[j], 0, ssd_in_w.shape[-1]) if nxt else None
            ssq_ctx = ssq_lat = None
        else:
            w_in = carried.pop("ssd_in")
            n_main = 2 * d_inner + 2 * N_GROUPS * D_STATE
            proj, _ = _matmul(u, w_in, BF16, n=n_main)
            dt_raw, _ = _matmul(u, w_in, F32, b_col0=n_main, n=2 * n_heads, bn=2 * n_heads)
            n_conv = n_main - d_inner
            d_exp = jnp.repeat(ssd_D[j], HEAD_DIM).reshape(1, d_inner)
            conv = lambda row0, n_rows, sq, rows: _conv_silu(proj, d_inner, n_conv, ssd_conv_w[j], ssd_conv_b[j], row0, n_rows, sq, rows)
            ssd = functools.partial(_ssd, proj=proj, dt_raw=dt_raw, dt_bias=ssd_dt_bias[j], a_log=ssd_A_log[j],
                                    d_skip_exp=d_exp, gnorm_w=ssd_norm_w[j], n_heads=n_heads)
            v_ctx, ssq_ctx, st = ssd(conv(0, n_ctx, seq, 4 * seq), h0=None, row0=0, n_seq=batch, seq=seq, want_state=True,
                                     nb=_TILES["ssd_seqs_ctx"])
            h0 = state_ssd[:, j].reshape(dec_batch, 2, d_inner, D_STATE)
            v_lat, ssq_lat = ssd(conv(n_ctx, n_lat, dec_seq, dec_seq), h0=h0, row0=n_ctx, n_seq=dec_batch, seq=dec_seq,
                                 want_state=False, nb=_TILES["ssd_seqs_lat"])
            new_states.append(st.reshape(batch, 2, n_heads, HEAD_DIM, D_STATE))
            w_out = carried.pop("ssd_out")

        h_ctx, carried["ssd_in"] = _matmul_residual(v_ctx, w_out, h_ctx, gate, 0, n_ctx, ssq_ctx, rider=out_rider)
        h_lat, _ = _matmul_residual(v_lat, w_out, h_lat, gate, 1, dec_seq, ssq_lat)

    y_prompt = _final_norm(h_ctx, final_norm_w).reshape(batch, seq, d)
    y_sample = _final_norm(h_lat, final_norm_w).reshape(dec_batch, dec_seq, d)
    return (y_prompt, y_sample, jnp.stack(new_states, axis=1))
```

```python
import functools

import jax
import jax.numpy as jnp
from jax import lax
from jax.experimental import pallas as pl
from jax.experimental.pallas import tpu as pltpu

F32 = jnp.float32
BF16 = jnp.bfloat16

EPS = 1e-6
POOL_WINDOWS = (2, 4, 8, 16)
GRID_W = 64
HEAD_DIM = 64
D_STATE = 128
N_GROUPS = 8
CONV_W = 7
CHUNK = 128
LANES = 128

V7X_VMEM_BYTES = 64 * 2**20
_VMEM_HEADROOM = 6 * 2**20

_TILES = dict(
    ada_bn=512,
    norm_rows=256,
    proj=(1024, 1024),
    fold=(512, 1024),
    out=(1024, 256, 8192),
    pool_ctx=(4, 1024),
    pool_lat=(1, 512),
    conv_cols=512,
    ssd_seqs_ctx=2,
    ssd_seqs_lat=1,
)


def _cparams(semantics, vmem_need):
    limit = min(max(int(vmem_need), 16 * 2**20), V7X_VMEM_BYTES - _VMEM_HEADROOM)
    return pltpu.CompilerParams(dimension_semantics=semantics, vmem_limit_bytes=limit)


def _silu(t):
    return t * jax.nn.sigmoid(t)


def _seg(tile, rows_per_tile, n_ctx, dec_seq):
    start = tile * rows_per_tile
    return jnp.where(start < n_ctx, 0, 1 + jnp.maximum(start - n_ctx, 0) // dec_seq)


def _ada_kernel(c_ref, w_ref, b_ref, o_ref):
    s = _silu(c_ref[...]).astype(BF16)
    o_ref[0] = jnp.dot(s, w_ref[0].astype(BF16), preferred_element_type=F32) + b_ref[0]


def _ada_mod(c_rows, ada_w, ada_b):
    depth, d, n3 = ada_w.shape
    rows = c_rows.shape[0]
    bn = min(_TILES["ada_bn"], n3)
    need = 2 * (d * bn * 4) + d * bn * 2 + 4 * rows * d * 4
    return pl.pallas_call(
        _ada_kernel,
        grid=(depth, n3 // bn),
        in_specs=[
            pl.BlockSpec((rows, d), lambda l, j: (0, 0)),
            pl.BlockSpec((1, d, bn), lambda l, j: (l, 0, j)),
            pl.BlockSpec((1, 1, bn), lambda l, j: (l, 0, j)),
        ],
        out_specs=pl.BlockSpec((1, rows, bn), lambda l, j: (l, 0, j)),
        out_shape=jax.ShapeDtypeStruct((depth, rows, n3), F32),
        compiler_params=_cparams(("arbitrary", "arbitrary"), need + 8 * 2**20),
        name="ada_mod",
    )(c_rows, ada_w, ada_b.reshape(depth, 1, n3))


def _norm_mod(x, w, shift, scale):
    xn = x * lax.rsqrt(jnp.mean(x * x, axis=-1, keepdims=True) + EPS) * w
    return xn * (1.0 + scale) + shift


def _prologue_kernel(xc_ref, xl_ref, w_ref, sh_ref, sc_ref, u_ref, *, ctx_tiles):
    x = jnp.where(pl.program_id(0) < ctx_tiles, xc_ref[...], xl_ref[...])
    u_ref[...] = _norm_mod(x, w_ref[...], sh_ref[0], sc_ref[0]).astype(BF16)


def _prologue(h_ctx, h_lat, w, shift, scale, dec_seq):
    n_ctx, d = h_ctx.shape
    t = n_ctx + h_lat.shape[0]
    tm = _TILES["norm_rows"]
    ctx_tiles = n_ctx // tm
    seg = lambda i: (_seg(i, tm, n_ctx, dec_seq), 0, 0)
    need = 2 * (2 * tm * d * 4 + tm * d * 2) + 6 * tm * d * 4
    return pl.pallas_call(
        functools.partial(_prologue_kernel, ctx_tiles=ctx_tiles),
        grid=(t // tm,),
        in_specs=[
            pl.BlockSpec((tm, d), lambda i: (jnp.minimum(i, ctx_tiles - 1), 0)),
            pl.BlockSpec((tm, d), lambda i: (jnp.maximum(i - ctx_tiles, 0), 0)),
            pl.BlockSpec((1, d), lambda i: (0, 0)),
            pl.BlockSpec((1, 1, d), seg),
            pl.BlockSpec((1, 1, d), seg),
        ],
        out_specs=pl.BlockSpec((tm, d), lambda i: (i, 0)),
        out_shape=jax.ShapeDtypeStruct((t, d), BF16),
        compiler_params=_cparams(("arbitrary",), need),
        name="prologue",
    )(h_ctx, h_lat, w, shift, scale)


def _rider_specs(rider, n_steps, step_of):
    src, col_block, n_cols = rider
    r = src.shape[0]
    rb = 16
    while r % rb or r // rb > n_steps:
        rb += 16
    last = r // rb - 1
    return (pl.BlockSpec((rb, n_cols), lambda *ids: (jnp.minimum(step_of(*ids), last), col_block)),
            pl.BlockSpec((rb, n_cols), lambda *ids: (jnp.minimum(step_of(*ids), last), 0)),
            jax.ShapeDtypeStruct((r, n_cols), BF16), 2 * rb * n_cols * (4 + 2))


def _mm_kernel(*refs, has_rider, b_index):
    a_ref, b_ref = refs[:2]
    o_ref = refs[3 if has_rider else 2]
    b = b_ref[...] if b_index is None else b_ref[b_index]
    o_ref[...] = jnp.dot(a_ref[...].astype(BF16), b.astype(BF16), preferred_element_type=F32).astype(o_ref.dtype)
    if has_rider:
        refs[4][...] = refs[2][...].astype(BF16)


def _matmul(a, b, out_dtype, b_col0=0, n=None, bn=None, rider=None):
    m, k = a.shape
    n = b.shape[1] if n is None else n
    bm = min(_TILES["proj"][0], m)
    bn = min(_TILES["proj"][1] if bn is None else bn, n)
    j0 = b_col0 // bn
    nj = n // bn
    osz = jnp.dtype(out_dtype).itemsize
    need = 2 * (bm * k * 2 + k * bn * 2 + bm * bn * osz) + bm * bn * 4
    in_specs = [pl.BlockSpec((bm, k), lambda i, j: (i, 0)), pl.BlockSpec((k, bn), lambda i, j: (0, j + j0))]
    out_specs = [pl.BlockSpec((bm, bn), lambda i, j: (i, j))]
    out_shape = [jax.ShapeDtypeStruct((m, n), out_dtype)]
    args = [a, b]
    if rider is not None:
        r_in, r_out, r_shape, r_bytes = _rider_specs(rider, (m // bm) * nj, lambda i, j: i * nj + j)
        in_specs.append(r_in), out_specs.append(r_out), out_shape.append(r_shape), args.append(rider[0])
        need += r_bytes
    res = pl.pallas_call(
        functools.partial(_mm_kernel, has_rider=rider is not None, b_index=None),
        grid=(m // bm, nj),
        in_specs=in_specs,
        out_specs=out_specs,
        out_shape=out_shape,
        compiler_params=_cparams(("arbitrary", "arbitrary"), need + 4 * 2**20),
        name="matmul",
    )(*args)
    return res[0], (res[1] if rider is not None else None)


def _matmul_grouped(a, b, out_dtype, rider=None):
    m = a.shape[0]
    groups, k, n = b.shape
    bm, bn = min(_TILES["fold"][0], m), min(_TILES["fold"][1], n)
    ni, nj = m // bm, n // bn
    osz = jnp.dtype(out_dtype).itemsize
    need = (2 * (bm * k * a.dtype.itemsize + k * bn * b.dtype.itemsize + bm * bn * osz)
            + bm * bn * 4 + (bm * k + k * bn) * 2)
    in_specs = [pl.BlockSpec((bm, k), lambda g, j, i: (i, g)), pl.BlockSpec((1, k, bn), lambda g, j, i: (g, 0, j))]
    out_specs = [pl.BlockSpec((bm, bn), lambda g, j, i: (i, g * nj + j))]
    out_shape = [jax.ShapeDtypeStruct((m, groups * n), out_dtype)]
    args = [a, b]
    if rider is not None:
        r_in, r_out, r_shape, r_bytes = _rider_specs(rider, groups * ni * nj, lambda g, j, i: (g * nj + j) * ni + i)
        in_specs.append(r_in), out_specs.append(r_out), out_shape.append(r_shape), args.append(rider[0])
        need += r_bytes
    res = pl.pallas_call(
        functools.partial(_mm_kernel, has_rider=rider is not None, b_index=0),
        grid=(groups, nj, ni),
        in_specs=in_specs,
        out_specs=out_specs,
        out_shape=out_shape,
        compiler_params=_cparams(("arbitrary", "arbitrary", "arbitrary"), need + 4 * 2**20),
        name="matmul_grouped",
    )(*args)
    return res[0], (res[1] if rider is not None else None)


def _mm_res_kernel(*refs, nk, norm_dim, has_rider):
    it = iter(refs)
    a_ref, b_ref, h_ref, g_ref = (next(it) for _ in range(4))
    ssq_ref = next(it) if norm_dim else None
    rsrc_ref = next(it) if has_rider else None
    o_ref = next(it)
    if has_rider:
        next(it)[...] = rsrc_ref[...].astype(BF16)

    def finish(acc):
        if norm_dim:
            ms = jnp.sum(ssq_ref[...], axis=1, keepdims=True) * (1.0 / norm_dim)
            acc = acc * lax.rsqrt(ms + EPS)
        o_ref[...] = h_ref[...] + g_ref[0] * acc

    if nk == 1:
        finish(jnp.dot(a_ref[...], b_ref[...], preferred_element_type=F32))
        return
    acc_ref = refs[-1]
    k = pl.program_id(2)

    @pl.when(k == 0)
    def _():
        acc_ref[...] = jnp.zeros_like(acc_ref)

    acc_ref[...] += jnp.dot(a_ref[...], b_ref[...], preferred_element_type=F32)

    @pl.when(k == nk - 1)
    def _():
        finish(acc_ref[...])


def _matmul_residual(a, b, h, gate, seg0, seg_rows, ssq=None, rider=None):
    m, kdim = a.shape
    n = b.shape[1]
    bm, bn, bk = (min(t, s) for t, s in zip(_TILES["out"], (m, n, kdim)))
    nj, nk = n // bn, kdim // bk
    need = 2 * (bm * bk * 2 + bk * bn * 2 + 2 * bm * bn * 4) + 2 * bm * bn * 4
    in_specs = [
        pl.BlockSpec((bm, bk), lambda i, j, k: (i, k)),
        pl.BlockSpec((bk, bn), lambda i, j, k: (k, j)),
        pl.BlockSpec((bm, bn), lambda i, j, k: (i, j)),
        pl.BlockSpec((1, 1, bn), lambda i, j, k: (seg0 + (i * bm) // seg_rows, 0, j)),
    ]
    args = [a, b, h, gate]
    if ssq is not None:
        in_specs.append(pl.BlockSpec((bm, ssq.shape[1]), lambda i, j, k: (i, 0)))
        args.append(ssq)
        need += 2 * bm * ssq.shape[1] * 4
    out_specs = [pl.BlockSpec((bm, bn), lambda i, j, k: (i, j))]
    out_shape = [jax.ShapeDtypeStruct((m, n), F32)]
    if rider is not None:
        r_in, r_out, r_shape, r_bytes = _rider_specs(rider, (m // bm) * nj * nk, lambda i, j, k: (i * nj + j) * nk + k)
        in_specs.append(r_in), out_specs.append(r_out), out_shape.append(r_shape), args.append(rider[0])
        need += r_bytes
    res = pl.pallas_call(
        functools.partial(_mm_res_kernel, nk=nk, norm_dim=kdim if ssq is not None else 0, has_rider=rider is not None),
        grid=(m // bm, nj, nk),
        in_specs=in_specs,
        out_specs=out_specs,
        out_shape=out_shape,
        scratch_shapes=[pltpu.VMEM((bm, bn), F32)] if nk > 1 else [],
        compiler_params=_cparams(("arbitrary", "arbitrary", "arbitrary"), need + 4 * 2**20),
        name="matmul_residual",
    )(*args)
    return res[0], (res[1] if rider is not None else None)


def _window_members(pos_t, pos_s, w):
    lo = pos_t - w // 2
    return (pos_s >= lo) & (pos_s < lo + w)


def _pool_membership(seq, grid_w):
    t = jnp.arange(seq, dtype=jnp.int32)[:, None]
    s = jnp.arange(seq, dtype=jnp.int32)[None, :]
    mats = []
    for w in POOL_WINDOWS:
        if grid_w is None:
            m = _window_members(t, s, w)
        else:
            m = _window_members(t // grid_w, s // grid_w, w) & _window_members(t % grid_w, s % grid_w, w)
        mats.append(m)
    return jnp.stack(mats).astype(BF16)


def _poolmix_kernel(y_ref, z_ref, p_ref, b_ref, sc_ref, o_ref, *, seq, nseq, rows_step, reaches, cols_per_group):
    bias = b_ref[...]
    scale = sc_ref[...]

    def mix(reach):
        for s in range(nseq):
            for r0 in range(0, seq, rows_step):
                k0, k1 = max(0, r0 - reach), min(seq, r0 + rows_step + reach)
                member = p_ref[0, pl.ds(r0, rows_step), pl.ds(k0, k1 - k0)]
                inv_cnt = 1.0 / jnp.sum(member.astype(F32), axis=1, keepdims=True)
                rows = pl.ds(s * seq + r0, rows_step)
                tot = jnp.dot(member, y_ref[pl.ds(s * seq + k0, k1 - k0), :], preferred_element_type=F32)
                mixed = tot * inv_cnt - y_ref[rows, :].astype(F32) + bias
                z = z_ref[rows, :].astype(F32)
                o_ref[rows, :] = (mixed * scale * _silu(z)).astype(BF16)

    if seq <= rows_step:
        mix(seq)
    else:
        group = pl.program_id(0) // cols_per_group
        for gi, reach in enumerate(reaches):
            pl.when(group == gi)(functools.partial(mix, reach))


def _poolmix(y, z, member, grp_b, scale, row0, n_rows, seq, tile, grid_w):
    e = y.shape[1]
    gw = e // len(POOL_WINDOWS)
    nseq, cb = tile[0], min(tile[1], gw)
    rows = seq * nseq
    r0 = row0 // rows
    rows_step = min(seq, 512)
    reaches = tuple(-(-(w // 2) * (grid_w or 1) // LANES) * LANES for w in POOL_WINDOWS)
    need = 2 * (3 * rows * cb * 2 + seq * seq * 2) + 6 * rows_step * cb * 4 + rows_step * seq * 4
    return pl.pallas_call(
        functools.partial(_poolmix_kernel, seq=seq, nseq=nseq, rows_step=rows_step, reaches=reaches, cols_per_group=gw // cb),
        grid=(e // cb, n_rows // rows),
        in_specs=[
            pl.BlockSpec((rows, cb), lambda c, i: (i + r0, c)),
            pl.BlockSpec((rows, cb), lambda c, i: (i + r0, c)),
            pl.BlockSpec((1, seq, seq), lambda c, i: (c * cb // gw, 0, 0)),
            pl.BlockSpec((1, cb), lambda c, i: (0, c)),
            pl.BlockSpec((1, cb), lambda c, i: (0, c)),
        ],
        out_specs=pl.BlockSpec((rows, cb), lambda c, i: (i, c)),
        out_shape=jax.ShapeDtypeStruct((n_rows, e), BF16),
        compiler_params=_cparams(("arbitrary", "arbitrary"), need + 8 * 2**20),
        name="poolmix",
    )(y, z, member, grp_b.reshape(1, e), scale.reshape(1, e))


_CONV_PAD = 16
_CONV_ROWS = 128


def _conv_kernel(x_ref, w_ref, b_ref, o_ref, xpad_ref, *, seq):
    rows, cb = x_ref.shape
    half = CONV_W // 2
    pitch = seq + _CONV_PAD
    win = _CONV_ROWS + 2 * _CONV_PAD
    taps = [k for k in range(CONV_W) if k != half]
    out_row = lax.broadcasted_iota(jnp.int32, (len(taps) * _CONV_ROWS, win), 0)
    src_row = lax.broadcasted_iota(jnp.int32, (len(taps) * _CONV_ROWS, win), 1)
    offset = sum(jnp.where(out_row // _CONV_ROWS == i, k - half, 0) for i, k in enumerate(taps))
    shift_all = (src_row == out_row % _CONV_ROWS + _CONV_PAD + offset).astype(BF16)

    for s in range(rows // seq + 1):
        xpad_ref[pl.ds(s * pitch, _CONV_PAD), :] = jnp.zeros((_CONV_PAD, cb), BF16)
    for s in range(rows // seq):
        xpad_ref[pl.ds(s * pitch + _CONV_PAD, seq), :] = x_ref[pl.ds(s * seq, seq), :]
    bias = b_ref[...]
    for s in range(rows // seq):
        for j in range(seq // _CONV_ROWS):
            base = s * pitch + _CONV_PAD + j * _CONV_ROWS
            shifted = jnp.dot(shift_all, xpad_ref[pl.ds(base - _CONV_PAD, win), :], preferred_element_type=F32)
            out_rows = pl.ds(s * seq + j * _CONV_ROWS, _CONV_ROWS)
            acc = bias + w_ref[pl.ds(half, 1), :] * x_ref[out_rows, :].astype(F32)
            for i, k in enumerate(taps):
                acc = acc + w_ref[pl.ds(k, 1), :] * shifted[i * _CONV_ROWS:(i + 1) * _CONV_ROWS]
            o_ref[out_rows, :] = _silu(acc).astype(BF16)


def _conv_silu(proj, col0, n_cols, conv_w, conv_b, row0, n_rows, seq, rows):
    assert rows % seq == 0 and seq % _CONV_ROWS == 0 and CONV_W // 2 <= _CONV_PAD
    cb = _TILES["conv_cols"]
    r0 = row0 // rows
    c0 = col0 // cb
    padded = (rows // seq) * (seq + _CONV_PAD) + _CONV_PAD
    need = 2 * (2 * rows * cb * 2) + padded * cb * 2 + 16 * _CONV_ROWS * cb * 4
    return pl.pallas_call(
        functools.partial(_conv_kernel, seq=seq),
        grid=(n_rows // rows, n_cols // cb),
        in_specs=[
            pl.BlockSpec((rows, cb), lambda i, c: (i + r0, c + c0)),
            pl.BlockSpec((CONV_W, cb), lambda i, c: (0, c)),
            pl.BlockSpec((1, cb), lambda i, c: (0, c)),
        ],
        out_specs=pl.BlockSpec((rows, cb), lambda i, c: (i, c)),
        out_shape=jax.ShapeDtypeStruct((n_rows, n_cols), BF16),
        scratch_shapes=[pltpu.VMEM((padded, cb), BF16)],
        compiler_params=_cparams(("arbitrary", "arbitrary"), need + 4 * 2**20),
        name="conv_silu",
    )(proj, conv_w, conv_b.reshape(1, n_cols))


_LOG2E = 1.4426950408889634


def _split3(v):
    hi = v.astype(BF16)
    r1 = v - hi.astype(F32)
    mid = r1.astype(BF16)
    lo = (r1 - mid.astype(F32)).astype(BF16)
    return hi, mid, lo


def _pack3(v, stride):
    hi, mid, lo = _split3(v)
    packed = hi.astype(F32) + pltpu.roll(mid.astype(F32), stride, axis=1) + pltpu.roll(lo.astype(F32), 2 * stride, axis=1)
    return packed.astype(BF16)


def _ssd_kernel(*refs, nb, nc, hpg, has_h0, want_state):
    it = iter(refs)
    x_ref, b_ref, c_ref, dt_ref, dtb_ref, alog_ref, dskip_ref, z_ref, gw_ref = (next(it) for _ in range(9))
    h0_ref = next(it) if has_h0 else None
    yg_ref, ssq_ref = next(it), next(it)
    st_ref = next(it) if want_state else None
    h_scr, y_scr, cum_scr, w3_scr, rowp_scr, bt_scr, s_scr, cd_scr = (next(it) for _ in range(8))

    q = CHUNK
    nct = nb * nc
    width = hpg * HEAD_DIM
    stride3 = 2 * hpg
    assert q == D_STATE == LANES and hpg % 2 == 0 and 2 * HEAD_DIM == LANES and 3 * stride3 <= LANES
    g = pl.program_id(1)
    shift0 = (LANES - hpg * g) % LANES
    shifts = (shift0, (shift0 + hpg) % LANES)

    lane = lax.broadcasted_iota(jnp.int32, (q, LANES), 1)
    lane_all = lax.broadcasted_iota(jnp.int32, (nct * q, LANES), 1)
    lane1 = lax.broadcasted_iota(jnp.int32, (1, LANES), 1)
    row_i = lax.broadcasted_iota(jnp.int32, (q, q), 0)
    col_i = lax.broadcasted_iota(jnp.int32, (q, q), 1)
    masks = (row_i >= col_i, row_i <= col_i)
    tri = tuple(m.astype(BF16) for m in masks)
    ones = jnp.ones((q, q), BF16)
    e_row = lax.broadcasted_iota(jnp.int32, (LANES, width), 0)
    e_head = lax.broadcasted_iota(jnp.int32, (LANES, width), 1) // HEAD_DIM
    expand = tuple(((e_row % stride3 == d * hpg + e_head) & (e_row < 3 * stride3)).astype(BF16) for d in range(2))
    first_half = lane < HEAD_DIM

    def pick(v0, v1, ln):
        return jnp.where(ln < hpg, v0, jnp.where(ln < 2 * hpg, v1, 0.0))

    def unpack3(p):
        s = p + pltpu.roll(p, LANES - stride3, axis=1) + pltpu.roll(p, LANES - 2 * stride3, axis=1)
        return jnp.where(lane_all < stride3, s, 0.0)

    for sl in range(nb):
        for d in range(2):
            if has_h0:
                h_scr[2 * sl + d] = jnp.transpose(h0_ref[sl, d])
            elif nc > 2:
                h_scr[2 * sl + d] = jnp.zeros((D_STATE, width), F32)

    neg_a = pick(*(-jnp.exp(pltpu.roll(alog_ref[pl.ds(d, 1), :], shifts[d], axis=1)) for d in range(2)), lane1)
    dt_bias = pick(*(pltpu.roll(dtb_ref[pl.ds(d, 1), :], shifts[d], axis=1) for d in range(2)), lane1)
    dt_raw = pick(*(pltpu.roll(dt_ref[:, pl.ds(d * LANES, LANES)], shifts[d], axis=1) for d in range(2)), lane_all)
    dt = jnp.where(lane_all < 2 * hpg, jax.nn.softplus(dt_raw + dt_bias), 0.0)
    a3 = _pack3(dt * neg_a, stride3)
    chunks = [a3[k * q:(k + 1) * q] for k in range(nct)]
    pfx, sfx, tot = (unpack3(jnp.concatenate([jnp.dot(m, ck, preferred_element_type=F32) for ck in chunks], axis=0))
                     for m in (tri[0], tri[1], ones))
    cum = jnp.where(lane_all < hpg, pfx, sfx)
    cum_scr[...] = cum * _LOG2E
    w3_scr[...] = _pack3(dt * jnp.exp(tot - cum), stride3)
    rowp = (cum - jnp.log(dt)) * _LOG2E
    decay8 = jnp.concatenate([jnp.exp(tot[k * q:k * q + 8]) for k in range(nct)], axis=0)
    lane8 = lax.broadcasted_iota(jnp.int32, (8 * nct, LANES), 1)
    cd3 = _pack3(jnp.where(lane8 < stride3, decay8, 0.0), stride3)
    for d in range(2):
        cd_scr[d] = jnp.dot(cd3, expand[d], preferred_element_type=F32)
    for k in range(nct):
        rows = pl.ds(k * q, q)
        rowp_scr[pl.ds(k * stride3, stride3), :] = jnp.transpose(rowp[k * q:(k + 1) * q])[:stride3, :]
        bc = b_ref[rows, :]
        bt_scr[rows, :] = jnp.transpose(bc.astype(F32)).astype(BF16)
        s_scr[rows, :] = lax.dot_general(c_ref[rows, :], bc, (((1,), (1,)), ((), ())), preferred_element_type=F32).astype(BF16)
    y_scr[...] = x_ref[...].astype(F32) * dskip_ref[...]

    def scan(i, carry, zero_state=False):
        for sl in range(nb):
            for d in range(2):
                k = sl * nc + (i if d == 0 else nc - 1 - i)
                rows = pl.ds(pl.multiple_of(k * q, q), q)
                cum_c = cum_scr[rows, :]
                scores = s_scr[rows, :]
                cc = c_ref[rows, :]
                if not zero_state:
                    h_in = h_scr[2 * sl + d]
                    h_bf = h_in.astype(BF16)
                for p in range(hpg // 2):
                    lanes = pl.ds(p * LANES, LANES)
                    rhs = x_ref[rows, lanes]
                    if not zero_state:
                        rhs = jnp.concatenate([rhs, h_bf[:, p * LANES:(p + 1) * LANES]], axis=0)
                    halves = []
                    for r in (2 * p, 2 * p + 1):
                        col = jnp.broadcast_to(cum_c[:, d * hpg + r:d * hpg + r + 1], (q, q))
                        row = rowp_scr[pl.ds(k * stride3 + d * hpg + r, 1), :]
                        lhs = scores * jnp.exp2(jnp.where(masks[d], col - row, -jnp.inf)).astype(BF16)
                        if not zero_state:
                            lhs = jnp.concatenate([lhs, cc * jnp.exp2(col).astype(BF16)], axis=1)
                        halves.append(jnp.dot(lhs, rhs, preferred_element_type=F32))
                    y_scr[rows, lanes] += jnp.where(first_half, halves[0], halves[1])
                w_exp = jnp.dot(w3_scr[rows, :], expand[d], preferred_element_type=F32)
                xd = (x_ref[rows, :].astype(F32) * w_exp).astype(BF16)
                states = jnp.dot(bt_scr[rows, :], xd, preferred_element_type=F32)
                if not zero_state:
                    states = h_in * cd_scr[d, pl.ds(pl.multiple_of(k * 8, 8), 1), :] + states
                h_scr[2 * sl + d] = states
        return carry

    if nc <= 2:
        for i in range(nc):
            scan(i, 0, zero_state=(i == 0 and not has_h0))
    else:
        lax.fori_loop(0, nc, scan, 0)

    def gate(k, carry):
        rows = pl.ds(pl.multiple_of(k * q, q), q)
        t = y_scr[rows, :] * _silu(z_ref[rows, :].astype(F32))
        yg_ref[rows, :] = (t * gw_ref[...]).astype(BF16)
        t2 = t * t
        ssq_ref[rows, :] = sum(t2[:, j * LANES:(j + 1) * LANES] for j in range(width // LANES))
        return carry

    lax.fori_loop(0, nct, gate, 0)

    if want_state:
        for sl in range(nb):
            for d in range(2):
                st_ref[sl, d] = jnp.transpose(h_scr[2 * sl + d])


def _ssd(xbc, proj, dt_raw, dt_bias, a_log, d_skip_exp, gnorm_w, h0, row0, n_seq, seq, n_heads, want_state, nb):
    d_inner = n_heads * HEAD_DIM
    hpg = n_heads // N_GROUPS
    width = hpg * HEAD_DIM
    nc = seq // CHUNK
    rows = nb * seq
    r0 = row0 // rows
    b0 = d_inner // D_STATE
    has_h0 = h0 is not None
    in_specs = [
        pl.BlockSpec((rows, width), lambda s, g: (s, g)),
        pl.BlockSpec((rows, D_STATE), lambda s, g: (s, b0 + g)),
        pl.BlockSpec((rows, D_STATE), lambda s, g: (s, b0 + N_GROUPS + g)),
        pl.BlockSpec((rows, 2 * n_heads), lambda s, g: (s + r0, 0)),
        pl.BlockSpec((2, n_heads), lambda s, g: (0, 0)),
        pl.BlockSpec((2, n_heads), lambda s, g: (0, 0)),
        pl.BlockSpec((1, width), lambda s, g: (0, g)),
        pl.BlockSpec((rows, width), lambda s, g: (s + r0, g)),
        pl.BlockSpec((1, width), lambda s, g: (0, g)),
    ]
    args = [xbc, xbc, xbc, dt_raw, dt_bias, a_log, d_skip_exp, proj, gnorm_w.reshape(1, d_inner)]
    if has_h0:
        in_specs.append(pl.BlockSpec((nb, 2, width, D_STATE), lambda s, g: (s, 0, g, 0)))
        args.append(h0)
    out_specs = [pl.BlockSpec((rows, width), lambda s, g: (s, g)), pl.BlockSpec((rows, LANES), lambda s, g: (s, g))]
    out_shape = [jax.ShapeDtypeStruct((n_seq * seq, d_inner), BF16), jax.ShapeDtypeStruct((n_seq * seq, N_GROUPS * LANES), F32)]
    if want_state:
        out_specs.append(pl.BlockSpec((nb, 2, width, D_STATE), lambda s, g: (s, 0, g, 0)))
        out_shape.append(jax.ShapeDtypeStruct((n_seq, 2, d_inner, D_STATE), F32))
    nct = nb * nc
    scratch = [
        pltpu.VMEM((2 * nb, D_STATE, width), F32),
        pltpu.VMEM((rows, width), F32),
        pltpu.VMEM((rows, LANES), F32),
        pltpu.VMEM((rows, LANES), BF16),
        pltpu.VMEM((nct * 2 * hpg, CHUNK), F32),
        pltpu.VMEM((rows, CHUNK), BF16),
        pltpu.VMEM((rows, CHUNK), BF16),
        pltpu.VMEM((2, 8 * nct, width), F32),
    ]
    need = (2 * (2 * rows * width * 2 + 2 * rows * D_STATE * 2 + rows * 2 * n_heads * 4 + rows * width * 2 + rows * LANES * 4)
            + ((4 if has_h0 else 0) + (4 if want_state else 0) + 2) * nb * width * D_STATE * 4
            + rows * width * 4 + rows * LANES * 12 + 16 * nct * width * 4 + 24 * CHUNK * width * 4)
    return pl.pallas_call(
        functools.partial(_ssd_kernel, nb=nb, nc=nc, hpg=hpg, has_h0=has_h0, want_state=want_state),
        grid=(n_seq // nb, N_GROUPS),
        in_specs=in_specs,
        out_specs=out_specs,
        out_shape=out_shape,
        scratch_shapes=scratch,
        compiler_params=_cparams(("arbitrary", "arbitrary"), need + 6 * 2**20),
        name="ssd_scan",
    )(*args)


def _final_norm_kernel(x_ref, w_ref, o_ref):
    x = x_ref[...]
    o_ref[...] = x * lax.rsqrt(jnp.mean(x * x, axis=-1, keepdims=True) + EPS) * w_ref[...]


def _final_norm(h, w):
    rows, d = h.shape
    tm = _TILES["norm_rows"]
    need = 2 * (2 * tm * d * 4) + 4 * tm * d * 4
    return pl.pallas_call(
        _final_norm_kernel,
        grid=(rows // tm,),
        in_specs=[pl.BlockSpec((tm, d), lambda i: (i, 0)), pl.BlockSpec((1, d), lambda i: (0, 0))],
        out_specs=pl.BlockSpec((tm, d), lambda i: (i, 0)),
        out_shape=jax.ShapeDtypeStruct((rows, d), F32),
        compiler_params=_cparams(("arbitrary",), need),
        name="final_norm",
    )(h, w.reshape(1, d))


def kernel(x_prompt, x_sample, state_ssd, c, c_ctx, ada_w, ada_b, norm_w, pool_in_w, pool_grp_w, pool_grp_b, pool_scale, pool_out_w, ssd_in_w, ssd_conv_w, ssd_conv_b, ssd_dt_bias, ssd_A_log, ssd_D, ssd_norm_w, ssd_out_w, final_norm_w):
    batch, seq, d = x_prompt.shape
    dec_batch, dec_seq, _ = x_sample.shape
    n_ctx, n_lat = batch * seq, dec_batch * dec_seq
    depth = ada_w.shape[0]
    n_heads = ssd_A_log.shape[-1]
    d_inner = n_heads * HEAD_DIM
    e_pool = pool_out_w.shape[1]

    c_rows = jnp.concatenate([c_ctx[None, :], c, jnp.zeros((8 - 1 - dec_batch, d), F32)], axis=0)
    mod = _ada_mod(c_rows, ada_w, ada_b).reshape(depth, 8, 3, d)
    nseg = 1 + dec_batch
    part = lambda layer, which: mod[layer, :nseg, which][:, None, :]

    h_ctx = x_prompt.reshape(n_ctx, d)
    h_lat = x_sample.reshape(n_lat, d)
    new_states = []
    carried = {}
    for layer in range(depth):
        j = layer // 2
        shift, scale, gate = (part(layer, k) for k in range(3))
        u = _prologue(h_ctx, h_lat, norm_w[layer].reshape(1, d), shift, scale, dec_seq)
        out_rider = None

        if layer % 2 == 0:
            nxt = layer + 1 < depth
            w_in = pool_in_w[j]
            w_fold, w_z = _matmul_grouped(w_in, pool_grp_w[j], BF16, rider=(w_in, 1, e_pool))
            y, w_out = _matmul(u, w_fold, BF16, rider=(pool_out_w[j], 0, d))
            z, carried["ssd_out"] = _matmul(u, w_z, BF16, rider=(ssd_out_w[j], 0, d) if nxt else None)
            v_ctx = _poolmix(y, z, _pool_membership(seq, None), pool_grp_b[j], pool_scale[j],
                             0, n_ctx, seq, _TILES["pool_ctx"], None)
            v_lat = _poolmix(y, z, _pool_membership(dec_seq, GRID_W), pool_grp_b[j], pool_scale[j],
                             n_ctx, n_lat, dec_seq, _TILES["pool_lat"], GRID_W)
            out_rider = (ssd_in_w[j], 0, ssd_in_w.shape[-1]) if nxt else None
            ssq_ctx = ssq_lat = None
        else:
            w_in = carried.pop("ssd_in")
            n_main = 2 * d_inner + 2 * N_GROUPS * D_STATE
            proj, _ = _matmul(u, w_in, BF16, n=n_main)
            dt_raw, _ = _matmul(u, w_in, F32, b_col0=n_main, n=2 * n_heads, bn=2 * n_heads)
            n_conv = n_main - d_inner
            d_exp = jnp.repeat(ssd_D[j], HEAD_DIM).reshape(1, d_inner)
            conv = lambda row0, n_rows, sq, rows: _conv_silu(proj, d_inner, n_conv, ssd_conv_w[j], ssd_conv_b[j], row0, n_rows, sq, rows)
            ssd = functools.partial(_ssd, proj=proj, dt_raw=dt_raw, dt_bias=ssd_dt_bias[j], a_log=ssd_A_log[j],
                                    d_skip_exp=d_exp, gnorm_w=ssd_norm_w[j], n_heads=n_heads)
            v_ctx, ssq_ctx, st = ssd(conv(0, n_ctx, seq, 4 * seq), h0=None, row0=0, n_seq=batch, seq=seq, want_state=True,
                                     nb=_TILES["ssd_seqs_ctx"])
            h0 = state_ssd[:, j].reshape(dec_batch, 2, d_inner, D_STATE)
            v_lat, ssq_lat = ssd(conv(n_ctx, n_lat, dec_seq, dec_seq), h0=h0, row0=n_ctx, n_seq=dec_batch, seq=dec_seq,
                                 want_state=False, nb=_TILES["ssd_seqs_lat"])
            new_states.append(st.reshape(batch, 2, n_heads, HEAD_DIM, D_STATE))
            w_out = carried.pop("ssd_out")

        h_ctx, carried["ssd_in"] = _matmul_residual(v_ctx, w_out, h_ctx, gate, 0, n_ctx, ssq_ctx, rider=out_rider)
        h_lat, _ = _matmul_residual(v_lat, w_out, h_lat, gate, 1, dec_seq, ssq_lat)

    y_prompt = _final_norm(h_ctx, final_norm_w).reshape(batch, seq, d)
    y_sample = _final_norm(h_lat, final_norm_w).reshape(dec_batch, dec_seq, d)
    return (y_prompt, y_sample, jnp.stack(new_states, axis=1))
```

```python
import functools

import jax
import jax.numpy as jnp
from jax import lax
from jax.experimental import pallas as pl
from jax.experimental.pallas import tpu as pltpu

F32 = jnp.float32
BF16 = jnp.bfloat16

EPS = 1e-6
POOL_WINDOWS = (2, 4, 8, 16)
GRID_W = 64
HEAD_DIM = 64
D_STATE = 128
N_GROUPS = 8
CONV_W = 7
CHUNK = 128
LANES = 128

V7X_VMEM_BYTES = 64 * 2**20
_VMEM_HEADROOM = 6 * 2**20

_TILES = dict(
    ada_bn=512,
    norm_rows=256,
    proj=(1024, 1024),
    fold=(512, 2048),
    out=(1024, 256, 8192),
    pool_ctx=(8, 1024),
    pool_lat=(1, 512),
    conv_cols=512,
    conv_rows=2048,
    ssd_seqs_ctx=2,
    ssd_seqs_lat=1,
)


def _cparams(semantics, vmem_need):
    limit = min(max(int(vmem_need), 16 * 2**20), V7X_VMEM_BYTES - _VMEM_HEADROOM)
    return pltpu.CompilerParams(dimension_semantics=semantics, vmem_limit_bytes=limit)


def _silu(t):
    return t * jax.nn.sigmoid(t)


def _seg(tile, rows_per_tile, n_ctx, dec_seq):
    start = tile * rows_per_tile
    return jnp.where(start < n_ctx, 0, 1 + jnp.maximum(start - n_ctx, 0) // dec_seq)


def _ada_kernel(c_ref, w_ref, b_ref, o_ref):
    s = _silu(c_ref[...]).astype(BF16)
    o_ref[0] = jnp.dot(s, w_ref[0].astype(BF16), preferred_element_type=F32) + b_ref[0]


def _ada_mod(c_rows, ada_w, ada_b):
    depth, d, n3 = ada_w.shape
    rows = c_rows.shape[0]
    bn = min(_TILES["ada_bn"], n3)
    need = 2 * (d * bn * 4) + d * bn * 2 + 4 * rows * d * 4
    return pl.pallas_call(
        _ada_kernel,
        grid=(depth, n3 // bn),
        in_specs=[
            pl.BlockSpec((rows, d), lambda l, j: (0, 0)),
            pl.BlockSpec((1, d, bn), lambda l, j: (l, 0, j)),
            pl.BlockSpec((1, 1, bn), lambda l, j: (l, 0, j)),
        ],
        out_specs=pl.BlockSpec((1, rows, bn), lambda l, j: (l, 0, j)),
        out_shape=jax.ShapeDtypeStruct((depth, rows, n3), F32),
        compiler_params=_cparams(("arbitrary", "arbitrary"), need + 8 * 2**20),
        name="ada_mod",
    )(c_rows, ada_w, ada_b.reshape(depth, 1, n3))


def _norm_mod(x, w, shift, scale):
    xn = x * lax.rsqrt(jnp.mean(x * x, axis=-1, keepdims=True) + EPS) * w
    return xn * (1.0 + scale) + shift


def _prologue_kernel(xc_ref, xl_ref, w_ref, sh_ref, sc_ref, u_ref, *, ctx_tiles):
    x = jnp.where(pl.program_id(0) < ctx_tiles, xc_ref[...], xl_ref[...])
    u_ref[...] = _norm_mod(x, w_ref[...], sh_ref[0], sc_ref[0]).astype(BF16)


def _prologue(h_ctx, h_lat, w, shift, scale, dec_seq):
    n_ctx, d = h_ctx.shape
    t = n_ctx + h_lat.shape[0]
    tm = _TILES["norm_rows"]
    ctx_tiles = n_ctx // tm
    seg = lambda i: (_seg(i, tm, n_ctx, dec_seq), 0, 0)
    need = 2 * (2 * tm * d * 4 + tm * d * 2) + 6 * tm * d * 4
    return pl.pallas_call(
        functools.partial(_prologue_kernel, ctx_tiles=ctx_tiles),
        grid=(t // tm,),
        in_specs=[
            pl.BlockSpec((tm, d), lambda i: (jnp.minimum(i, ctx_tiles - 1), 0)),
            pl.BlockSpec((tm, d), lambda i: (jnp.maximum(i - ctx_tiles, 0), 0)),
            pl.BlockSpec((1, d), lambda i: (0, 0)),
            pl.BlockSpec((1, 1, d), seg),
            pl.BlockSpec((1, 1, d), seg),
        ],
        out_specs=pl.BlockSpec((tm, d), lambda i: (i, 0)),
        out_shape=jax.ShapeDtypeStruct((t, d), BF16),
        compiler_params=_cparams(("arbitrary",), need),
        name="prologue",
    )(h_ctx, h_lat, w, shift, scale)


def _rider_specs(rider, n_steps, step_of):
    src, col_block, n_cols = rider
    r = src.shape[0]
    rb = 16
    while r % rb or r // rb > n_steps:
        rb += 16
    last = r // rb - 1
    return (pl.BlockSpec((rb, n_cols), lambda *ids: (jnp.minimum(step_of(*ids), last), col_block)),
            pl.BlockSpec((rb, n_cols), lambda *ids: (jnp.minimum(step_of(*ids), last), 0)),
            jax.ShapeDtypeStruct((r, n_cols), BF16), 2 * rb * n_cols * (4 + 2))


def _mm_kernel(*refs, has_rider, b_index):
    a_ref, b_ref = refs[:2]
    o_ref = refs[3 if has_rider else 2]
    b = b_ref[...] if b_index is None else b_ref[b_index]
    o_ref[...] = jnp.dot(a_ref[...].astype(BF16), b.astype(BF16), preferred_element_type=F32).astype(o_ref.dtype)
    if has_rider:
        refs[4][...] = refs[2][...].astype(BF16)


def _matmul(a, b, out_dtype, b_col0=0, n=None, bn=None, rider=None):
    m, k = a.shape
    n = b.shape[1] if n is None else n
    bm = min(_TILES["proj"][0], m)
    bn = min(_TILES["proj"][1] if bn is None else bn, n)
    j0 = b_col0 // bn
    nj = n // bn
    osz = jnp.dtype(out_dtype).itemsize
    need = 2 * (bm * k * 2 + k * bn * 2 + bm * bn * osz) + bm * bn * 4
    in_specs = [pl.BlockSpec((bm, k), lambda i, j: (i, 0)), pl.BlockSpec((k, bn), lambda i, j: (0, j + j0))]
    out_specs = [pl.BlockSpec((bm, bn), lambda i, j: (i, j))]
    out_shape = [jax.ShapeDtypeStruct((m, n), out_dtype)]
    args = [a, b]
    if rider is not None:
        r_in, r_out, r_shape, r_bytes = _rider_specs(rider, (m // bm) * nj, lambda i, j: i * nj + j)
        in_specs.append(r_in), out_specs.append(r_out), out_shape.append(r_shape), args.append(rider[0])
        need += r_bytes
    res = pl.pallas_call(
        functools.partial(_mm_kernel, has_rider=rider is not None, b_index=None),
        grid=(m // bm, nj),
        in_specs=in_specs,
        out_specs=out_specs,
        out_shape=out_shape,
        compiler_params=_cparams(("arbitrary", "arbitrary"), need + 4 * 2**20),
        name="matmul",
    )(*args)
    return res[0], (res[1] if rider is not None else None)


def _matmul_grouped(a, b, out_dtype, rider=None):
    m = a.shape[0]
    groups, k, n = b.shape
    bm, bn = min(_TILES["fold"][0], m), min(_TILES["fold"][1], n)
    ni, nj = m // bm, n // bn
    osz = jnp.dtype(out_dtype).itemsize
    need = (2 * (bm * k * a.dtype.itemsize + k * bn * b.dtype.itemsize + bm * bn * osz)
            + bm * bn * 4 + (bm * k + k * bn) * 2)
    in_specs = [pl.BlockSpec((bm, k), lambda g, j, i: (i, g)), pl.BlockSpec((1, k, bn), lambda g, j, i: (g, 0, j))]
    out_specs = [pl.BlockSpec((bm, bn), lambda g, j, i: (i, g * nj + j))]
    out_shape = [jax.ShapeDtypeStruct((m, groups * n), out_dtype)]
    args = [a, b]
    if rider is not None:
        r_in, r_out, r_shape, r_bytes = _rider_specs(rider, groups * ni * nj, lambda g, j, i: (g * nj + j) * ni + i)
        in_specs.append(r_in), out_specs.append(r_out), out_shape.append(r_shape), args.append(rider[0])
        need += r_bytes
    res = pl.pallas_call(
        functools.partial(_mm_kernel, has_rider=rider is not None, b_index=0),
        grid=(groups, nj, ni),
        in_specs=in_specs,
        out_specs=out_specs,
        out_shape=out_shape,
        compiler_params=_cparams(("arbitrary", "arbitrary", "arbitrary"), need + 4 * 2**20),
        name="matmul_grouped",
    )(*args)
    return res[0], (res[1] if rider is not None else None)


def _mm_res_kernel(*refs, nk, norm_dim, has_rider):
    it = iter(refs)
    a_ref, b_ref, h_ref, g_ref = (next(it) for _ in range(4))
    ssq_ref = next(it) if norm_dim else None
    rsrc_ref = next(it) if has_rider else None
    o_ref = next(it)
    if has_rider:
        next(it)[...] = rsrc_ref[...].astype(BF16)

    def finish(acc):
        if norm_dim:
            ms = jnp.sum(ssq_ref[...], axis=1, keepdims=True) * (1.0 / norm_dim)
            acc = acc * lax.rsqrt(ms + EPS)
        o_ref[...] = h_ref[...] + g_ref[0] * acc

    if nk == 1:
        finish(jnp.dot(a_ref[...], b_ref[...], preferred_element_type=F32))
        return
    acc_ref = refs[-1]
    k = pl.program_id(2)

    @pl.when(k == 0)
    def _():
        acc_ref[...] = jnp.zeros_like(acc_ref)

    acc_ref[...] += jnp.dot(a_ref[...], b_ref[...], preferred_element_type=F32)

    @pl.when(k == nk - 1)
    def _():
        finish(acc_ref[...])


def _matmul_residual(a, b, h, gate, seg0, seg_rows, ssq=None, rider=None):
    m, kdim = a.shape
    n = b.shape[1]
    bm, bn, bk = (min(t, s) for t, s in zip(_TILES["out"], (m, n, kdim)))
    nj, nk = n // bn, kdim // bk
    need = 2 * (bm * bk * 2 + bk * bn * 2 + 2 * bm * bn * 4) + 2 * bm * bn * 4
    in_specs = [
        pl.BlockSpec((bm, bk), lambda i, j, k: (i, k)),
        pl.BlockSpec((bk, bn), lambda i, j, k: (k, j)),
        pl.BlockSpec((bm, bn), lambda i, j, k: (i, j)),
        pl.BlockSpec((1, 1, bn), lambda i, j, k: (seg0 + (i * bm) // seg_rows, 0, j)),
    ]
    args = [a, b, h, gate]
    if ssq is not None:
        in_specs.append(pl.BlockSpec((bm, ssq.shape[1]), lambda i, j, k: (i, 0)))
        args.append(ssq)
        need += 2 * bm * ssq.shape[1] * 4
    out_specs = [pl.BlockSpec((bm, bn), lambda i, j, k: (i, j))]
    out_shape = [jax.ShapeDtypeStruct((m, n), F32)]
    if rider is not None:
        r_in, r_out, r_shape, r_bytes = _rider_specs(rider, (m // bm) * nj * nk, lambda i, j, k: (i * nj + j) * nk + k)
        in_specs.append(r_in), out_specs.append(r_out), out_shape.append(r_shape), args.append(rider[0])
        need += r_bytes
    res = pl.pallas_call(
        functools.partial(_mm_res_kernel, nk=nk, norm_dim=kdim if ssq is not None else 0, has_rider=rider is not None),
        grid=(m // bm, nj, nk),
        in_specs=in_specs,
        out_specs=out_specs,
        out_shape=out_shape,
        scratch_shapes=[pltpu.VMEM((bm, bn), F32)] if nk > 1 else [],
        compiler_params=_cparams(("arbitrary", "arbitrary", "arbitrary"), need + 4 * 2**20),
        name="matmul_residual",
    )(*args)
    return res[0], (res[1] if rider is not None else None)


def _window_members(pos_t, pos_s, w):
    lo = pos_t - w // 2
    return (pos_s >= lo) & (pos_s < lo + w)


def _pool_membership(seq, grid_w):
    t = jnp.arange(seq, dtype=jnp.int32)[:, None]
    s = jnp.arange(seq, dtype=jnp.int32)[None, :]
    mats = []
    for w in POOL_WINDOWS:
        if grid_w is None:
            m = _window_members(t, s, w)
        else:
            m = _window_members(t // grid_w, s // grid_w, w) & _window_members(t % grid_w, s % grid_w, w)
        mats.append(m)
    return jnp.stack(mats).astype(BF16)


def _poolmix_kernel(y_ref, z_ref, p_ref, b_ref, sc_ref, o_ref, *, seq, nseq, rows_step, reaches, cols_per_group):
    bias = b_ref[...]
    scale = sc_ref[...]

    def mix(reach):
        for s in range(nseq):
            for r0 in range(0, seq, rows_step):
                k0, k1 = max(0, r0 - reach), min(seq, r0 + rows_step + reach)
                member = p_ref[0, pl.ds(r0, rows_step), pl.ds(k0, k1 - k0)]
                inv_cnt = 1.0 / jnp.sum(member.astype(F32), axis=1, keepdims=True)
                rows = pl.ds(s * seq + r0, rows_step)
                tot = jnp.dot(member, y_ref[pl.ds(s * seq + k0, k1 - k0), :], preferred_element_type=F32)
                mixed = tot * inv_cnt - y_ref[rows, :].astype(F32) + bias
                z = z_ref[rows, :].astype(F32)
                o_ref[rows, :] = (mixed * scale * _silu(z)).astype(BF16)

    if seq <= rows_step:
        mix(seq)
    else:
        group = pl.program_id(0) // cols_per_group
        for gi, reach in enumerate(reaches):
            pl.when(group == gi)(functools.partial(mix, reach))


def _poolmix(y, z, member, grp_b, scale, row0, n_rows, seq, tile, grid_w):
    e = y.shape[1]
    gw = e // len(POOL_WINDOWS)
    nseq, cb = min(tile[0], n_rows // seq), min(tile[1], gw)
    rows = seq * nseq
    r0 = row0 // rows
    rows_step = min(seq, 512)
    reaches = tuple(-(-(w // 2) * (grid_w or 1) // LANES) * LANES for w in POOL_WINDOWS)
    need = 2 * (3 * rows * cb * 2 + seq * seq * 2) + 6 * rows_step * cb * 4 + rows_step * seq * 4
    return pl.pallas_call(
        functools.partial(_poolmix_kernel, seq=seq, nseq=nseq, rows_step=rows_step, reaches=reaches, cols_per_group=gw // cb),
        grid=(e // cb, n_rows // rows),
        in_specs=[
            pl.BlockSpec((rows, cb), lambda c, i: (i + r0, c)),
            pl.BlockSpec((rows, cb), lambda c, i: (i + r0, c)),
            pl.BlockSpec((1, seq, seq), lambda c, i: (c * cb // gw, 0, 0)),
            pl.BlockSpec((1, cb), lambda c, i: (0, c)),
            pl.BlockSpec((1, cb), lambda c, i: (0, c)),
        ],
        out_specs=pl.BlockSpec((rows, cb), lambda c, i: (i, c)),
        out_shape=jax.ShapeDtypeStruct((n_rows, e), BF16),
        compiler_params=_cparams(("arbitrary", "arbitrary"), need + 8 * 2**20),
        name="poolmix",
    )(y, z, member, grp_b.reshape(1, e), scale.reshape(1, e))


_CONV_PAD = 16
_CONV_ROWS = 128


def _conv_kernel(x_ref, w_ref, b_ref, o_ref, xpad_ref, *, seq):
    rows, cb = x_ref.shape
    half = CONV_W // 2
    pitch = seq + _CONV_PAD
    win = _CONV_ROWS + 2 * _CONV_PAD
    taps = [k for k in range(CONV_W) if k != half]
    out_row = lax.broadcasted_iota(jnp.int32, (len(taps) * _CONV_ROWS, win), 0)
    src_row = lax.broadcasted_iota(jnp.int32, (len(taps) * _CONV_ROWS, win), 1)
    offset = sum(jnp.where(out_row // _CONV_ROWS == i, k - half, 0) for i, k in enumerate(taps))
    shift_all = (src_row == out_row % _CONV_ROWS + _CONV_PAD + offset).astype(BF16)

    for s in range(rows // seq + 1):
        xpad_ref[pl.ds(s * pitch, _CONV_PAD), :] = jnp.zeros((_CONV_PAD, cb), BF16)
    for s in range(rows // seq):
        xpad_ref[pl.ds(s * pitch + _CONV_PAD, seq), :] = x_ref[pl.ds(s * seq, seq), :]
    bias = b_ref[...]
    for s in range(rows // seq):
        for j in range(seq // _CONV_ROWS):
            base = s * pitch + _CONV_PAD + j * _CONV_ROWS
            shifted = jnp.dot(shift_all, xpad_ref[pl.ds(base - _CONV_PAD, win), :], preferred_element_type=F32)
            out_rows = pl.ds(s * seq + j * _CONV_ROWS, _CONV_ROWS)
            acc = bias + w_ref[pl.ds(half, 1), :] * x_ref[out_rows, :].astype(F32)
            for i, k in enumerate(taps):
                acc = acc + w_ref[pl.ds(k, 1), :] * shifted[i * _CONV_ROWS:(i + 1) * _CONV_ROWS]
            o_ref[out_rows, :] = _silu(acc).astype(BF16)


def _conv_silu(proj, col0, n_cols, conv_w, conv_b, row0, n_rows, seq, rows):
    assert rows % seq == 0 and seq % _CONV_ROWS == 0 and CONV_W // 2 <= _CONV_PAD
    cb = _TILES["conv_cols"]
    r0 = row0 // rows
    c0 = col0 // cb
    padded = (rows // seq) * (seq + _CONV_PAD) + _CONV_PAD
    need = 2 * (2 * rows * cb * 2) + padded * cb * 2 + 16 * _CONV_ROWS * cb * 4
    return pl.pallas_call(
        functools.partial(_conv_kernel, seq=seq),
        grid=(n_rows // rows, n_cols // cb),
        in_specs=[
            pl.BlockSpec((rows, cb), lambda i, c: (i + r0, c + c0)),
            pl.BlockSpec((CONV_W, cb), lambda i, c: (0, c)),
            pl.BlockSpec((1, cb), lambda i, c: (0, c)),
        ],
        out_specs=pl.BlockSpec((rows, cb), lambda i, c: (i, c)),
        out_shape=jax.ShapeDtypeStruct((n_rows, n_cols), BF16),
        scratch_shapes=[pltpu.VMEM((padded, cb), BF16)],
        compiler_params=_cparams(("arbitrary", "arbitrary"), need + 4 * 2**20),
        name="conv_silu",
    )(proj, conv_w, conv_b.reshape(1, n_cols))


_LOG2E = 1.4426950408889634


def _split3(v):
    hi = v.astype(BF16)
    r1 = v - hi.astype(F32)
    mid = r1.astype(BF16)
    lo = (r1 - mid.astype(F32)).astype(BF16)
    return hi, mid, lo


def _pack3(v, stride):
    hi, mid, lo = _split3(v)
    packed = hi.astype(F32) + pltpu.roll(mid.astype(F32), stride, axis=1) + pltpu.roll(lo.astype(F32), 2 * stride, axis=1)
    return packed.astype(BF16)


def _ssd_kernel(*refs, nb, nc, hpg, has_h0, want_state):
    it = iter(refs)
    x_ref, b_ref, c_ref, dt_ref, dtb_ref, alog_ref, dskip_ref, z_ref, gw_ref = (next(it) for _ in range(9))
    h0_ref = next(it) if has_h0 else None
    yg_ref, ssq_ref = next(it), next(it)
    st_ref = next(it) if want_state else None
    h_scr, y_scr, cum_scr, w3_scr, rowp_scr, bt_scr, s_scr, cd_scr = (next(it) for _ in range(8))

    q = CHUNK
    nct = nb * nc
    width = hpg * HEAD_DIM
    stride3 = 2 * hpg
    assert q == D_STATE == LANES and hpg % 2 == 0 and 2 * HEAD_DIM == LANES and 3 * stride3 <= LANES
    g = pl.program_id(1)
    shift0 = (LANES - hpg * g) % LANES
    shifts = (shift0, (shift0 + hpg) % LANES)

    lane = lax.broadcasted_iota(jnp.int32, (q, LANES), 1)
    lane_all = lax.broadcasted_iota(jnp.int32, (nct * q, LANES), 1)
    lane1 = lax.broadcasted_iota(jnp.int32, (1, LANES), 1)
    row_i = lax.broadcasted_iota(jnp.int32, (q, q), 0)
    col_i = lax.broadcasted_iota(jnp.int32, (q, q), 1)
    masks = (row_i >= col_i, row_i <= col_i)
    tri = tuple(m.astype(BF16) for m in masks)
    ones = jnp.ones((q, q), BF16)
    e_row = lax.broadcasted_iota(jnp.int32, (LANES, width), 0)
    e_head = lax.broadcasted_iota(jnp.int32, (LANES, width), 1) // HEAD_DIM
    expand = tuple(((e_row % stride3 == d * hpg + e_head) & (e_row < 3 * stride3)).astype(BF16) for d in range(2))
    first_half = lane < HEAD_DIM

    def pick(v0, v1, ln):
        return jnp.where(ln < hpg, v0, jnp.where(ln < 2 * hpg, v1, 0.0))

    def unpack3(p):
        s = p + pltpu.roll(p, LANES - stride3, axis=1) + pltpu.roll(p, LANES - 2 * stride3, axis=1)
        return jnp.where(lane_all < stride3, s, 0.0)

    for sl in range(nb):
        for d in range(2):
            if has_h0:
                h_scr[2 * sl + d] = jnp.transpose(h0_ref[sl, d])
            elif nc > 2:
                h_scr[2 * sl + d] = jnp.zeros((D_STATE, width), F32)

    neg_a = pick(*(-jnp.exp(pltpu.roll(alog_ref[pl.ds(d, 1), :], shifts[d], axis=1)) for d in range(2)), lane1)
    dt_bias = pick(*(pltpu.roll(dtb_ref[pl.ds(d, 1), :], shifts[d], axis=1) for d in range(2)), lane1)
    dt_raw = pick(*(pltpu.roll(dt_ref[:, pl.ds(d * LANES, LANES)], shifts[d], axis=1) for d in range(2)), lane_all)
    dt = jnp.where(lane_all < 2 * hpg, jax.nn.softplus(dt_raw + dt_bias), 0.0)
    a3 = _pack3(dt * neg_a, stride3)
    chunks = [a3[k * q:(k + 1) * q] for k in range(nct)]
    pfx, sfx, tot = (unpack3(jnp.concatenate([jnp.dot(m, ck, preferred_element_type=F32) for ck in chunks], axis=0))
                     for m in (tri[0], tri[1], ones))
    cum = jnp.where(lane_all < hpg, pfx, sfx)
    cum_scr[...] = cum * _LOG2E
    w3_scr[...] = _pack3(dt * jnp.exp(tot - cum), stride3)
    rowp = (cum - jnp.log(dt)) * _LOG2E
    decay8 = jnp.concatenate([jnp.exp(tot[k * q:k * q + 8]) for k in range(nct)], axis=0)
    lane8 = lax.broadcasted_iota(jnp.int32, (8 * nct, LANES), 1)
    cd3 = _pack3(jnp.where(lane8 < stride3, decay8, 0.0), stride3)
    for d in range(2):
        cd_scr[d] = jnp.dot(cd3, expand[d], preferred_element_type=F32)
    for k in range(nct):
        rows = pl.ds(k * q, q)
        rowp_scr[pl.ds(k * stride3, stride3), :] = jnp.transpose(rowp[k * q:(k + 1) * q])[:stride3, :]
        bc = b_ref[rows, :]
        bt_scr[rows, :] = jnp.transpose(bc.astype(F32)).astype(BF16)
        s_scr[rows, :] = lax.dot_general(c_ref[rows, :], bc, (((1,), (1,)), ((), ())), preferred_element_type=F32).astype(BF16)
    y_scr[...] = x_ref[...].astype(F32) * dskip_ref[...]

    def scan(i, carry, zero_state=False):
        for sl in range(nb):
            for d in range(2):
                k = sl * nc + (i if d == 0 else nc - 1 - i)
                rows = pl.ds(pl.multiple_of(k * q, q), q)
                cum_c = cum_scr[rows, :]
                scores = s_scr[rows, :]
                cc = c_ref[rows, :]
                if not zero_state:
                    h_in = h_scr[2 * sl + d]
                    h_bf = h_in.astype(BF16)
                for p in range(hpg // 2):
                    lanes = pl.ds(p * LANES, LANES)
                    rhs = x_ref[rows, lanes]
                    if not zero_state:
                        rhs = jnp.concatenate([rhs, h_bf[:, p * LANES:(p + 1) * LANES]], axis=0)
                    halves = []
                    for r in (2 * p, 2 * p + 1):
                        col = jnp.broadcast_to(cum_c[:, d * hpg + r:d * hpg + r + 1], (q, q))
                        row = rowp_scr[pl.ds(k * stride3 + d * hpg + r, 1), :]
                        lhs = scores * jnp.exp2(jnp.where(masks[d], col - row, -jnp.inf)).astype(BF16)
                        if not zero_state:
                            lhs = jnp.concatenate([lhs, cc * jnp.exp2(col).astype(BF16)], axis=1)
                        halves.append(jnp.dot(lhs, rhs, preferred_element_type=F32))
                    y_scr[rows, lanes] += jnp.where(first_half, halves[0], halves[1])
                w_exp = jnp.dot(w3_scr[rows, :], expand[d], preferred_element_type=F32)
                xd = (x_ref[rows, :].astype(F32) * w_exp).astype(BF16)
                states = jnp.dot(bt_scr[rows, :], xd, preferred_element_type=F32)
                if not zero_state:
                    states = h_in * cd_scr[d, pl.ds(pl.multiple_of(k * 8, 8), 1), :] + states
                h_scr[2 * sl + d] = states
        return carry

    if nc <= 2:
        for i in range(nc):
            scan(i, 0, zero_state=(i == 0 and not has_h0))
    else:
        lax.fori_loop(0, nc, scan, 0, unroll=2)

    def gate(k, carry):
        rows = pl.ds(pl.multiple_of(k * q, q), q)
        t = y_scr[rows, :] * _silu(z_ref[rows, :].astype(F32))
        yg_ref[rows, :] = (t * gw_ref[...]).astype(BF16)
        t2 = t * t
        ssq_ref[rows, :] = sum(t2[:, j * LANES:(j + 1) * LANES] for j in range(width // LANES))
        return carry

    lax.fori_loop(0, nct, gate, 0)

    if want_state:
        for sl in range(nb):
            for d in range(2):
                st_ref[sl, d] = jnp.transpose(h_scr[2 * sl + d])


def _ssd(xbc, proj, dt_raw, dt_bias, a_log, d_skip_exp, gnorm_w, h0, row0, n_seq, seq, n_heads, want_state, nb):
    d_inner = n_heads * HEAD_DIM
    hpg = n_heads // N_GROUPS
    width = hpg * HEAD_DIM
    nc = seq // CHUNK
    rows = nb * seq
    r0 = row0 // rows
    b0 = d_inner // D_STATE
    has_h0 = h0 is not None
    in_specs = [
        pl.BlockSpec((rows, width), lambda s, g: (s, g)),
        pl.BlockSpec((rows, D_STATE), lambda s, g: (s, b0 + g)),
        pl.BlockSpec((rows, D_STATE), lambda s, g: (s, b0 + N_GROUPS + g)),
        pl.BlockSpec((rows, 2 * n_heads), lambda s, g: (s + r0, 0)),
        pl.BlockSpec((2, n_heads), lambda s, g: (0, 0)),
        pl.BlockSpec((2, n_heads), lambda s, g: (0, 0)),
        pl.BlockSpec((1, width), lambda s, g: (0, g)),
        pl.BlockSpec((rows, width), lambda s, g: (s + r0, g)),
        pl.BlockSpec((1, width), lambda s, g: (0, g)),
    ]
    args = [xbc, xbc, xbc, dt_raw, dt_bias, a_log, d_skip_exp, proj, gnorm_w.reshape(1, d_inner)]
    if has_h0:
        in_specs.append(pl.BlockSpec((nb, 2, width, D_STATE), lambda s, g: (s, 0, g, 0)))
        args.append(h0)
    out_specs = [pl.BlockSpec((rows, width), lambda s, g: (s, g)), pl.BlockSpec((rows, LANES), lambda s, g: (s, g))]
    out_shape = [jax.ShapeDtypeStruct((n_seq * seq, d_inner), BF16), jax.ShapeDtypeStruct((n_seq * seq, N_GROUPS * LANES), F32)]
    if want_state:
        out_specs.append(pl.BlockSpec((nb, 2, width, D_STATE), lambda s, g: (s, 0, g, 0)))
        out_shape.append(jax.ShapeDtypeStruct((n_seq, 2, d_inner, D_STATE), F32))
    nct = nb * nc
    scratch = [
        pltpu.VMEM((2 * nb, D_STATE, width), F32),
        pltpu.VMEM((rows, width), F32),
        pltpu.VMEM((rows, LANES), F32),
        pltpu.VMEM((rows, LANES), BF16),
        pltpu.VMEM((nct * 2 * hpg, CHUNK), F32),
        pltpu.VMEM((rows, CHUNK), BF16),
        pltpu.VMEM((rows, CHUNK), BF16),
        pltpu.VMEM((2, 8 * nct, width), F32),
    ]
    need = (2 * (2 * rows * width * 2 + 2 * rows * D_STATE * 2 + rows * 2 * n_heads * 4 + rows * width * 2 + rows * LANES * 4)
            + ((4 if has_h0 else 0) + (4 if want_state else 0) + 2) * nb * width * D_STATE * 4
            + rows * width * 4 + rows * LANES * 12 + 16 * nct * width * 4 + 24 * CHUNK * width * 4)
    return pl.pallas_call(
        functools.partial(_ssd_kernel, nb=nb, nc=nc, hpg=hpg, has_h0=has_h0, want_state=want_state),
        grid=(n_seq // nb, N_GROUPS),
        in_specs=in_specs,
        out_specs=out_specs,
        out_shape=out_shape,
        scratch_shapes=scratch,
        compiler_params=_cparams(("arbitrary", "arbitrary"), need + 6 * 2**20),
        name="ssd_scan",
    )(*args)


def _final_norm_kernel(x_ref, w_ref, o_ref):
    x = x_ref[...]
    o_ref[...] = x * lax.rsqrt(jnp.mean(x * x, axis=-1, keepdims=True) + EPS) * w_ref[...]


def _final_norm(h, w):
    rows, d = h.shape
    tm = _TILES["norm_rows"]
    need = 2 * (2 * tm * d * 4) + 4 * tm * d * 4
    return pl.pallas_call(
        _final_norm_kernel,
        grid=(rows // tm,),
        in_specs=[pl.BlockSpec((tm, d), lambda i: (i, 0)), pl.BlockSpec((1, d), lambda i: (0, 0))],
        out_specs=pl.BlockSpec((tm, d), lambda i: (i, 0)),
        out_shape=jax.ShapeDtypeStruct((rows, d), F32),
        compiler_params=_cparams(("arbitrary",), need),
        name="final_norm",
    )(h, w.reshape(1, d))


def kernel(x_prompt, x_sample, state_ssd, c, c_ctx, ada_w, ada_b, norm_w, pool_in_w, pool_grp_w, pool_grp_b, pool_scale, pool_out_w, ssd_in_w, ssd_conv_w, ssd_conv_b, ssd_dt_bias, ssd_A_log, ssd_D, ssd_norm_w, ssd_out_w, final_norm_w):
    batch, seq, d = x_prompt.shape
    dec_batch, dec_seq, _ = x_sample.shape
    n_ctx, n_lat = batch * seq, dec_batch * dec_seq
    depth = ada_w.shape[0]
    n_heads = ssd_A_log.shape[-1]
    d_inner = n_heads * HEAD_DIM
    e_pool = pool_out_w.shape[1]

    c_rows = jnp.concatenate([c_ctx[None, :], c, jnp.zeros((8 - 1 - dec_batch, d), F32)], axis=0)
    mod = _ada_mod(c_rows, ada_w, ada_b).reshape(depth, 8, 3, d)
    nseg = 1 + dec_batch
    part = lambda layer, which: mod[layer, :nseg, which][:, None, :]

    h_ctx = x_prompt.reshape(n_ctx, d)
    h_lat = x_sample.reshape(n_lat, d)
    new_states = []
    carried = {}
    for layer in range(depth):
        j = layer // 2
        shift, scale, gate = (part(layer, k) for k in range(3))
        u = _prologue(h_ctx, h_lat, norm_w[layer].reshape(1, d), shift, scale, dec_seq)
        out_rider = None

        if layer % 2 == 0:
            nxt = layer + 1 < depth
            w_in = pool_in_w[j]
            w_fold, w_z = _matmul_grouped(w_in, pool_grp_w[j], BF16, rider=(w_in, 1, e_pool))
            y, w_out = _matmul(u, w_fold, BF16, rider=(pool_out_w[j], 0, d))
            z, carried["ssd_out"] = _matmul(u, w_z, BF16, rider=(ssd_out_w[j], 0, d) if nxt else None)
            v_ctx = _poolmix(y, z, _pool_membership(seq, None), pool_grp_b[j], pool_scale[j],
                             0, n_ctx, seq, _TILES["pool_ctx"], None)
            v_lat = _poolmix(y, z, _pool_membership(dec_seq, GRID_W), pool_grp_b[j], pool_scale[j],
                             n_ctx, n_lat, dec_seq, _TILES["pool_lat"], GRID_W)
            out_rider = (ssd_in_w[j], 0, ssd_in_w.shape[-1]) if nxt else None
            ssq_ctx = ssq_lat = None
        else:
            w_in = carried.pop("ssd_in")
            n_main = 2 * d_inner + 2 * N_GROUPS * D_STATE
            proj, _ = _matmul(u, w_in, BF16, n=n_main)
            dt_raw, _ = _matmul(u, w_in, F32, b_col0=n_main, n=2 * n_heads, bn=2 * n_heads)
            n_conv = n_main - d_inner
            d_exp = jnp.repeat(ssd_D[j], HEAD_DIM).reshape(1, d_inner)
            conv = lambda row0, n_rows, sq, rows: _conv_silu(proj, d_inner, n_conv, ssd_conv_w[j], ssd_conv_b[j], row0, n_rows, sq, rows)
            ssd = functools.partial(_ssd, proj=proj, dt_raw=dt_raw, dt_bias=ssd_dt_bias[j], a_log=ssd_A_log[j],
                                    d_skip_exp=d_exp, gnorm_w=ssd_norm_w[j], n_heads=n_heads)
            conv_rows = max(seq, min(_TILES["conv_rows"], n_ctx))
            v_ctx, ssq_ctx, st = ssd(conv(0, n_ctx, seq, conv_rows), h0=None, row0=0, n_seq=batch, seq=seq, want_state=True,
                                     nb=_TILES["ssd_seqs_ctx"])
            h0 = state_ssd[:, j].reshape(dec_batch, 2, d_inner, D_STATE)
            v_lat, ssq_lat = ssd(conv(n_ctx, n_lat, dec_seq, dec_seq), h0=h0, row0=n_ctx, n_seq=dec_batch, seq=dec_seq,
                                 want_state=False, nb=_TILES["ssd_seqs_lat"])
            new_states.append(st.reshape(batch, 2, n_heads, HEAD_DIM, D_STATE))
            w_out = carried.pop("ssd_out")

        h_ctx, carried["ssd_in"] = _matmul_residual(v_ctx, w_out, h_ctx, gate, 0, n_ctx, ssq_ctx, rider=out_rider)
        h_lat, _ = _matmul_residual(v_lat, w_out, h_lat, gate, 1, dec_seq, ssq_lat)

    y_prompt = _final_norm(h_ctx, final_norm_w).reshape(batch, seq, d)
    y_sample = _final_norm(h_lat, final_norm_w).reshape(dec_batch, dec_seq, d)
    return (y_prompt, y_sample, jnp.stack(new_states, axis=1))
```

```python
import functools

import jax
import jax.numpy as jnp
from jax import lax
from jax.experimental import pallas as pl
from jax.experimental.pallas import tpu as pltpu

F32 = jnp.float32
BF16 = jnp.bfloat16

EPS = 1e-6
POOL_WINDOWS = (2, 4, 8, 16)
GRID_W = 64
HEAD_DIM = 64
D_STATE = 128
N_GROUPS = 8
CONV_W = 7
CHUNK = 128
LANES = 128

V7X_VMEM_BYTES = 64 * 2**20
_VMEM_HEADROOM = 6 * 2**20

_TILES = dict(
    ada_bn=512,
    norm_rows=256,
    proj=(1024, 1024),
    fold=(512, 2048),
    out=(1024, 256, 8192),
    pool_ctx=(8, 1024),
    pool_lat=(1, 512),
    conv_cols=512,
    conv_rows=2048,
    ssd_seqs_ctx=4,
    ssd_seqs_lat=1,
)


def _cparams(semantics, vmem_need):
    limit = min(max(int(vmem_need), 16 * 2**20), V7X_VMEM_BYTES - _VMEM_HEADROOM)
    return pltpu.CompilerParams(dimension_semantics=semantics, vmem_limit_bytes=limit)


def _silu(t):
    return t * jax.nn.sigmoid(t)


def _seg(tile, rows_per_tile, n_ctx, dec_seq):
    start = tile * rows_per_tile
    return jnp.where(start < n_ctx, 0, 1 + jnp.maximum(start - n_ctx, 0) // dec_seq)


def _ada_kernel(c_ref, w_ref, b_ref, o_ref):
    s = _silu(c_ref[...]).astype(BF16)
    o_ref[0] = jnp.dot(s, w_ref[0].astype(BF16), preferred_element_type=F32) + b_ref[0]


def _ada_mod(c_rows, ada_w, ada_b):
    depth, d, n3 = ada_w.shape
    rows = c_rows.shape[0]
    bn = min(_TILES["ada_bn"], n3)
    need = 2 * (d * bn * 4) + d * bn * 2 + 4 * rows * d * 4
    return pl.pallas_call(
        _ada_kernel,
        grid=(depth, n3 // bn),
        in_specs=[
            pl.BlockSpec((rows, d), lambda l, j: (0, 0)),
            pl.BlockSpec((1, d, bn), lambda l, j: (l, 0, j)),
            pl.BlockSpec((1, 1, bn), lambda l, j: (l, 0, j)),
        ],
        out_specs=pl.BlockSpec((1, rows, bn), lambda l, j: (l, 0, j)),
        out_shape=jax.ShapeDtypeStruct((depth, rows, n3), F32),
        compiler_params=_cparams(("arbitrary", "arbitrary"), need + 8 * 2**20),
        name="ada_mod",
    )(c_rows, ada_w, ada_b.reshape(depth, 1, n3))


def _norm_mod(x, w, shift, scale):
    xn = x * lax.rsqrt(jnp.mean(x * x, axis=-1, keepdims=True) + EPS) * w
    return xn * (1.0 + scale) + shift


def _prologue_kernel(xc_ref, xl_ref, w_ref, sh_ref, sc_ref, u_ref, *, ctx_tiles):
    x = jnp.where(pl.program_id(0) < ctx_tiles, xc_ref[...], xl_ref[...])
    u_ref[...] = _norm_mod(x, w_ref[...], sh_ref[0], sc_ref[0]).astype(BF16)


def _prologue(h_ctx, h_lat, w, shift, scale, dec_seq):
    n_ctx, d = h_ctx.shape
    t = n_ctx + h_lat.shape[0]
    tm = _TILES["norm_rows"]
    ctx_tiles = n_ctx // tm
    seg = lambda i: (_seg(i, tm, n_ctx, dec_seq), 0, 0)
    need = 2 * (2 * tm * d * 4 + tm * d * 2) + 6 * tm * d * 4
    return pl.pallas_call(
        functools.partial(_prologue_kernel, ctx_tiles=ctx_tiles),
        grid=(t // tm,),
        in_specs=[
            pl.BlockSpec((tm, d), lambda i: (jnp.minimum(i, ctx_tiles - 1), 0)),
            pl.BlockSpec((tm, d), lambda i: (jnp.maximum(i - ctx_tiles, 0), 0)),
            pl.BlockSpec((1, d), lambda i: (0, 0)),
            pl.BlockSpec((1, 1, d), seg),
            pl.BlockSpec((1, 1, d), seg),
        ],
        out_specs=pl.BlockSpec((tm, d), lambda i: (i, 0)),
        out_shape=jax.ShapeDtypeStruct((t, d), BF16),
        compiler_params=_cparams(("arbitrary",), need),
        name="prologue",
    )(h_ctx, h_lat, w, shift, scale)


def _rider_specs(rider, n_steps, step_of):
    src, col_block, n_cols = rider
    r = src.shape[0]
    rb = 16
    while r % rb or r // rb > n_steps:
        rb += 16
    last = r // rb - 1
    return (pl.BlockSpec((rb, n_cols), lambda *ids: (jnp.minimum(step_of(*ids), last), col_block)),
            pl.BlockSpec((rb, n_cols), lambda *ids: (jnp.minimum(step_of(*ids), last), 0)),
            jax.ShapeDtypeStruct((r, n_cols), BF16), 2 * rb * n_cols * (4 + 2))


def _mm_kernel(*refs, has_rider, b_index):
    a_ref, b_ref = refs[:2]
    o_ref = refs[3 if has_rider else 2]
    b = b_ref[...] if b_index is None else b_ref[b_index]
    o_ref[...] = jnp.dot(a_ref[...].astype(BF16), b.astype(BF16), preferred_element_type=F32).astype(o_ref.dtype)
    if has_rider:
        refs[4][...] = refs[2][...].astype(BF16)


def _matmul(a, b, out_dtype, b_col0=0, n=None, bn=None, rider=None):
    m, k = a.shape
    n = b.shape[1] if n is None else n
    bm = min(_TILES["proj"][0], m)
    bn = min(_TILES["proj"][1] if bn is None else bn, n)
    j0 = b_col0 // bn
    nj = n // bn
    osz = jnp.dtype(out_dtype).itemsize
    need = 2 * (bm * k * 2 + k * bn * 2 + bm * bn * osz) + bm * bn * 4
    in_specs = [pl.BlockSpec((bm, k), lambda i, j: (i, 0)), pl.BlockSpec((k, bn), lambda i, j: (0, j + j0))]
    out_specs = [pl.BlockSpec((bm, bn), lambda i, j: (i, j))]
    out_shape = [jax.ShapeDtypeStruct((m, n), out_dtype)]
    args = [a, b]
    if rider is not None:
        r_in, r_out, r_shape, r_bytes = _rider_specs(rider, (m // bm) * nj, lambda i, j: i * nj + j)
        in_specs.append(r_in), out_specs.append(r_out), out_shape.append(r_shape), args.append(rider[0])
        need += r_bytes
    res = pl.pallas_call(
        functools.partial(_mm_kernel, has_rider=rider is not None, b_index=None),
        grid=(m // bm, nj),
        in_specs=in_specs,
        out_specs=out_specs,
        out_shape=out_shape,
        compiler_params=_cparams(("arbitrary", "arbitrary"), need + 4 * 2**20),
        name="matmul",
    )(*args)
    return res[0], (res[1] if rider is not None else None)


def _matmul_grouped(a, b, out_dtype, rider=None):
    m = a.shape[0]
    groups, k, n = b.shape
    bm, bn = min(_TILES["fold"][0], m), min(_TILES["fold"][1], n)
    ni, nj = m // bm, n // bn
    osz = jnp.dtype(out_dtype).itemsize
    need = (2 * (bm * k * a.dtype.itemsize + k * bn * b.dtype.itemsize + bm * bn * osz)
            + bm * bn * 4 + (bm * k + k * bn) * 2)
    in_specs = [pl.BlockSpec((bm, k), lambda g, j, i: (i, g)), pl.BlockSpec((1, k, bn), lambda g, j, i: (g, 0, j))]
    out_specs = [pl.BlockSpec((bm, bn), lambda g, j, i: (i, g * nj + j))]
    out_shape = [jax.ShapeDtypeStruct((m, groups * n), out_dtype)]
    args = [a, b]
    if rider is not None:
        r_in, r_out, r_shape, r_bytes = _rider_specs(rider, groups * ni * nj, lambda g, j, i: (g * nj + j) * ni + i)
        in_specs.append(r_in), out_specs.append(r_out), out_shape.append(r_shape), args.append(rider[0])
        need += r_bytes
    res = pl.pallas_call(
        functools.partial(_mm_kernel, has_rider=rider is not None, b_index=0),
        grid=(groups, nj, ni),
        in_specs=in_specs,
        out_specs=out_specs,
        out_shape=out_shape,
        compiler_params=_cparams(("arbitrary", "arbitrary", "arbitrary"), need + 4 * 2**20),
        name="matmul_grouped",
    )(*args)
    return res[0], (res[1] if rider is not None else None)


def _mm_res_kernel(*refs, nk, norm_dim, has_rider):
    it = iter(refs)
    a_ref, b_ref, h_ref, g_ref = (next(it) for _ in range(4))
    ssq_ref = next(it) if norm_dim else None
    rsrc_ref = next(it) if has_rider else None
    o_ref = next(it)
    if has_rider:
        next(it)[...] = rsrc_ref[...].astype(BF16)

    def finish(acc):
        if norm_dim:
            ms = jnp.sum(ssq_ref[...], axis=1, keepdims=True) * (1.0 / norm_dim)
            acc = acc * lax.rsqrt(ms + EPS)
        o_ref[...] = h_ref[...] + g_ref[0] * acc

    if nk == 1:
        finish(jnp.dot(a_ref[...], b_ref[...], preferred_element_type=F32))
        return
    acc_ref = refs[-1]
    k = pl.program_id(2)

    @pl.when(k == 0)
    def _():
        acc_ref[...] = jnp.zeros_like(acc_ref)

    acc_ref[...] += jnp.dot(a_ref[...], b_ref[...], preferred_element_type=F32)

    @pl.when(k == nk - 1)
    def _():
        finish(acc_ref[...])


def _matmul_residual(a, b, h, gate, seg0, seg_rows, ssq=None, rider=None):
    m, kdim = a.shape
    n = b.shape[1]
    bm, bn, bk = (min(t, s) for t, s in zip(_TILES["out"], (m, n, kdim)))
    nj, nk = n // bn, kdim // bk
    need = 2 * (bm * bk * 2 + bk * bn * 2 + 2 * bm * bn * 4) + 2 * bm * bn * 4
    in_specs = [
        pl.BlockSpec((bm, bk), lambda i, j, k: (i, k)),
        pl.BlockSpec((bk, bn), lambda i, j, k: (k, j)),
        pl.BlockSpec((bm, bn), lambda i, j, k: (i, j)),
        pl.BlockSpec((1, 1, bn), lambda i, j, k: (seg0 + (i * bm) // seg_rows, 0, j)),
    ]
    args = [a, b, h, gate]
    if ssq is not None:
        in_specs.append(pl.BlockSpec((bm, ssq.shape[1]), lambda i, j, k: (i, 0)))
        args.append(ssq)
        need += 2 * bm * ssq.shape[1] * 4
    out_specs = [pl.BlockSpec((bm, bn), lambda i, j, k: (i, j))]
    out_shape = [jax.ShapeDtypeStruct((m, n), F32)]
    if rider is not None:
        r_in, r_out, r_shape, r_bytes = _rider_specs(rider, (m // bm) * nj * nk, lambda i, j, k: (i * nj + j) * nk + k)
        in_specs.append(r_in), out_specs.append(r_out), out_shape.append(r_shape), args.append(rider[0])
        need += r_bytes
    res = pl.pallas_call(
        functools.partial(_mm_res_kernel, nk=nk, norm_dim=kdim if ssq is not None else 0, has_rider=rider is not None),
        grid=(m // bm, nj, nk),
        in_specs=in_specs,
        out_specs=out_specs,
        out_shape=out_shape,
        scratch_shapes=[pltpu.VMEM((bm, bn), F32)] if nk > 1 else [],
        compiler_params=_cparams(("arbitrary", "arbitrary", "arbitrary"), need + 4 * 2**20),
        name="matmul_residual",
    )(*args)
    return res[0], (res[1] if rider is not None else None)


def _window_members(pos_t, pos_s, w):
    lo = pos_t - w // 2
    return (pos_s >= lo) & (pos_s < lo + w)


def _pool_membership(seq, grid_w):
    t = jnp.arange(seq, dtype=jnp.int32)[:, None]
    s = jnp.arange(seq, dtype=jnp.int32)[None, :]
    mats = []
    for w in POOL_WINDOWS:
        if grid_w is None:
            m = _window_members(t, s, w)
        else:
            m = _window_members(t // grid_w, s // grid_w, w) & _window_members(t % grid_w, s % grid_w, w)
        mats.append(m)
    return jnp.stack(mats).astype(BF16)


def _poolmix_kernel(y_ref, z_ref, p_ref, b_ref, sc_ref, o_ref, *, seq, nseq, rows_step, reaches, cols_per_group):
    bias = b_ref[...]
    scale = sc_ref[...]

    def mix(reach):
        for s in range(nseq):
            for r0 in range(0, seq, rows_step):
                k0, k1 = max(0, r0 - reach), min(seq, r0 + rows_step + reach)
                member = p_ref[0, pl.ds(r0, rows_step), pl.ds(k0, k1 - k0)]
                inv_cnt = 1.0 / jnp.sum(member.astype(F32), axis=1, keepdims=True)
                rows = pl.ds(s * seq + r0, rows_step)
                tot = jnp.dot(member, y_ref[pl.ds(s * seq + k0, k1 - k0), :], preferred_element_type=F32)
                mixed = tot * inv_cnt - y_ref[rows, :].astype(F32) + bias
                z = z_ref[rows, :].astype(F32)
                o_ref[rows, :] = (mixed * scale * _silu(z)).astype(BF16)

    if seq <= rows_step:
        mix(seq)
    else:
        group = pl.program_id(0) // cols_per_group
        for gi, reach in enumerate(reaches):
            pl.when(group == gi)(functools.partial(mix, reach))


def _poolmix(y, z, member, grp_b, scale, row0, n_rows, seq, tile, grid_w):
    e = y.shape[1]
    gw = e // len(POOL_WINDOWS)
    nseq, cb = min(tile[0], n_rows // seq), min(tile[1], gw)
    rows = seq * nseq
    r0 = row0 // rows
    rows_step = min(seq, 512)
    reaches = tuple(-(-(w // 2) * (grid_w or 1) // LANES) * LANES for w in POOL_WINDOWS)
    need = 2 * (3 * rows * cb * 2 + seq * seq * 2) + 6 * rows_step * cb * 4 + rows_step * seq * 4
    return pl.pallas_call(
        functools.partial(_poolmix_kernel, seq=seq, nseq=nseq, rows_step=rows_step, reaches=reaches, cols_per_group=gw // cb),
        grid=(e // cb, n_rows // rows),
        in_specs=[
            pl.BlockSpec((rows, cb), lambda c, i: (i + r0, c)),
            pl.BlockSpec((rows, cb), lambda c, i: (i + r0, c)),
            pl.BlockSpec((1, seq, seq), lambda c, i: (c * cb // gw, 0, 0)),
            pl.BlockSpec((1, cb), lambda c, i: (0, c)),
            pl.BlockSpec((1, cb), lambda c, i: (0, c)),
        ],
        out_specs=pl.BlockSpec((rows, cb), lambda c, i: (i, c)),
        out_shape=jax.ShapeDtypeStruct((n_rows, e), BF16),
        compiler_params=_cparams(("arbitrary", "arbitrary"), need + 8 * 2**20),
        name="poolmix",
    )(y, z, member, grp_b.reshape(1, e), scale.reshape(1, e))


_CONV_PAD = 16
_CONV_ROWS = 128


def _conv_kernel(x_ref, w_ref, b_ref, o_ref, xpad_ref, *, seq):
    rows, cb = x_ref.shape
    half = CONV_W // 2
    pitch = seq + _CONV_PAD
    win = _CONV_ROWS + 2 * _CONV_PAD
    taps = [k for k in range(CONV_W) if k != half]
    out_row = lax.broadcasted_iota(jnp.int32, (len(taps) * _CONV_ROWS, win), 0)
    src_row = lax.broadcasted_iota(jnp.int32, (len(taps) * _CONV_ROWS, win), 1)
    offset = sum(jnp.where(out_row // _CONV_ROWS == i, k - half, 0) for i, k in enumerate(taps))
    shift_all = (src_row == out_row % _CONV_ROWS + _CONV_PAD + offset).astype(BF16)

    for s in range(rows // seq + 1):
        xpad_ref[pl.ds(s * pitch, _CONV_PAD), :] = jnp.zeros((_CONV_PAD, cb), BF16)
    for s in range(rows // seq):
        xpad_ref[pl.ds(s * pitch + _CONV_PAD, seq), :] = x_ref[pl.ds(s * seq, seq), :]
    bias = b_ref[...]
    for s in range(rows // seq):
        for j in range(seq // _CONV_ROWS):
            base = s * pitch + _CONV_PAD + j * _CONV_ROWS
            shifted = jnp.dot(shift_all, xpad_ref[pl.ds(base - _CONV_PAD, win), :], preferred_element_type=F32)
            out_rows = pl.ds(s * seq + j * _CONV_ROWS, _CONV_ROWS)
            acc = bias + w_ref[pl.ds(half, 1), :] * x_ref[out_rows, :].astype(F32)
            for i, k in enumerate(taps):
                acc = acc + w_ref[pl.ds(k, 1), :] * shifted[i * _CONV_ROWS:(i + 1) * _CONV_ROWS]
            o_ref[out_rows, :] = _silu(acc).astype(BF16)


def _conv_silu(proj, col0, n_cols, conv_w, conv_b, row0, n_rows, seq, rows):
    assert rows % seq == 0 and seq % _CONV_ROWS == 0 and CONV_W // 2 <= _CONV_PAD
    cb = _TILES["conv_cols"]
    r0 = row0 // rows
    c0 = col0 // cb
    padded = (rows // seq) * (seq + _CONV_PAD) + _CONV_PAD
    need = 2 * (2 * rows * cb * 2) + padded * cb * 2 + 16 * _CONV_ROWS * cb * 4
    return pl.pallas_call(
        functools.partial(_conv_kernel, seq=seq),
        grid=(n_rows // rows, n_cols // cb),
        in_specs=[
            pl.BlockSpec((rows, cb), lambda i, c: (i + r0, c + c0)),
            pl.BlockSpec((CONV_W, cb), lambda i, c: (0, c)),
            pl.BlockSpec((1, cb), lambda i, c: (0, c)),
        ],
        out_specs=pl.BlockSpec((rows, cb), lambda i, c: (i, c)),
        out_shape=jax.ShapeDtypeStruct((n_rows, n_cols), BF16),
        scratch_shapes=[pltpu.VMEM((padded, cb), BF16)],
        compiler_params=_cparams(("arbitrary", "arbitrary"), need + 4 * 2**20),
        name="conv_silu",
    )(proj, conv_w, conv_b.reshape(1, n_cols))


_LOG2E = 1.4426950408889634


def _split3(v):
    hi = v.astype(BF16)
    r1 = v - hi.astype(F32)
    mid = r1.astype(BF16)
    lo = (r1 - mid.astype(F32)).astype(BF16)
    return hi, mid, lo


def _pack3(v, stride):
    hi, mid, lo = _split3(v)
    packed = hi.astype(F32) + pltpu.roll(mid.astype(F32), stride, axis=1) + pltpu.roll(lo.astype(F32), 2 * stride, axis=1)
    return packed.astype(BF16)


def _ssd_kernel(*refs, nb, nc, hpg, has_h0, want_state):
    it = iter(refs)
    x_ref, b_ref, c_ref, dt_ref, dtb_ref, alog_ref, dskip_ref, z_ref, gw_ref = (next(it) for _ in range(9))
    h0_ref = next(it) if has_h0 else None
    yg_ref, ssq_ref = next(it), next(it)
    st_ref = next(it) if want_state else None
    h_scr, y_scr, cum_scr, w3_scr, rowp_scr, bt_scr, s_scr, cd_scr = (next(it) for _ in range(8))

    q = CHUNK
    nct = nb * nc
    width = hpg * HEAD_DIM
    stride3 = 2 * hpg
    assert q == D_STATE == LANES and hpg % 2 == 0 and 2 * HEAD_DIM == LANES and 3 * stride3 <= LANES
    g = pl.program_id(1)
    shift0 = (LANES - hpg * g) % LANES
    shifts = (shift0, (shift0 + hpg) % LANES)

    lane = lax.broadcasted_iota(jnp.int32, (q, LANES), 1)
    lane_all = lax.broadcasted_iota(jnp.int32, (nct * q, LANES), 1)
    lane1 = lax.broadcasted_iota(jnp.int32, (1, LANES), 1)
    row_i = lax.broadcasted_iota(jnp.int32, (q, q), 0)
    col_i = lax.broadcasted_iota(jnp.int32, (q, q), 1)
    masks = (row_i >= col_i, row_i <= col_i)
    tri = tuple(m.astype(BF16) for m in masks)
    ones = jnp.ones((q, q), BF16)
    e_row = lax.broadcasted_iota(jnp.int32, (LANES, width), 0)
    e_head = lax.broadcasted_iota(jnp.int32, (LANES, width), 1) // HEAD_DIM
    expand = tuple(((e_row % stride3 == d * hpg + e_head) & (e_row < 3 * stride3)).astype(BF16) for d in range(2))
    first_half = lane < HEAD_DIM

    def pick(v0, v1, ln):
        return jnp.where(ln < hpg, v0, jnp.where(ln < 2 * hpg, v1, 0.0))

    def unpack3(p):
        s = p + pltpu.roll(p, LANES - stride3, axis=1) + pltpu.roll(p, LANES - 2 * stride3, axis=1)
        return jnp.where(lane_all < stride3, s, 0.0)

    for sl in range(nb):
        for d in range(2):
            if has_h0:
                h_scr[2 * sl + d] = jnp.transpose(h0_ref[sl, d])
            elif nc > 2:
                h_scr[2 * sl + d] = jnp.zeros((D_STATE, width), F32)

    neg_a = pick(*(-jnp.exp(pltpu.roll(alog_ref[pl.ds(d, 1), :], shifts[d], axis=1)) for d in range(2)), lane1)
    dt_bias = pick(*(pltpu.roll(dtb_ref[pl.ds(d, 1), :], shifts[d], axis=1) for d in range(2)), lane1)
    dt_raw = pick(*(pltpu.roll(dt_ref[:, pl.ds(d * LANES, LANES)], shifts[d], axis=1) for d in range(2)), lane_all)
    dt = jnp.where(lane_all < 2 * hpg, jax.nn.softplus(dt_raw + dt_bias), 0.0)
    a3 = _pack3(dt * neg_a, stride3)
    chunks = [a3[k * q:(k + 1) * q] for k in range(nct)]
    pfx, sfx, tot = (unpack3(jnp.concatenate([jnp.dot(m, ck, preferred_element_type=F32) for ck in chunks], axis=0))
                     for m in (tri[0], tri[1], ones))
    cum = jnp.where(lane_all < hpg, pfx, sfx)
    cum_scr[...] = cum * _LOG2E
    w3_scr[...] = _pack3(dt * jnp.exp(tot - cum), stride3)
    rowp = (cum - jnp.log(dt)) * _LOG2E
    decay8 = jnp.concatenate([jnp.exp(tot[k * q:k * q + 8]) for k in range(nct)], axis=0)
    lane8 = lax.broadcasted_iota(jnp.int32, (8 * nct, LANES), 1)
    cd3 = _pack3(jnp.where(lane8 < stride3, decay8, 0.0), stride3)
    for d in range(2):
        cd_scr[d] = jnp.dot(cd3, expand[d], preferred_element_type=F32)
    for k in range(nct):
        rows = pl.ds(k * q, q)
        rowp_scr[pl.ds(k * stride3, stride3), :] = jnp.transpose(rowp[k * q:(k + 1) * q])[:stride3, :]
        bc = b_ref[rows, :]
        bt_scr[rows, :] = jnp.transpose(bc.astype(F32)).astype(BF16)
        s_scr[rows, :] = lax.dot_general(c_ref[rows, :], bc, (((1,), (1,)), ((), ())), preferred_element_type=F32).astype(BF16)
    y_scr[...] = x_ref[...].astype(F32) * dskip_ref[...]

    def scan(i, carry, zero_state=False):
        for sl in range(nb):
            for d in range(2):
                k = sl * nc + (i if d == 0 else nc - 1 - i)
                rows = pl.ds(pl.multiple_of(k * q, q), q)
                cum_c = cum_scr[rows, :]
                scores = s_scr[rows, :]
                cc = c_ref[rows, :]
                if not zero_state:
                    h_in = h_scr[2 * sl + d]
                    h_bf = h_in.astype(BF16)
                for p in range(hpg // 2):
                    lanes = pl.ds(p * LANES, LANES)
                    rhs = x_ref[rows, lanes]
                    if not zero_state:
                        rhs = jnp.concatenate([rhs, h_bf[:, p * LANES:(p + 1) * LANES]], axis=0)
                    halves = []
                    for r in (2 * p, 2 * p + 1):
                        col = jnp.broadcast_to(cum_c[:, d * hpg + r:d * hpg + r + 1], (q, q))
                        row = rowp_scr[pl.ds(k * stride3 + d * hpg + r, 1), :]
                        lhs = scores * jnp.exp2(jnp.where(masks[d], col - row, -jnp.inf)).astype(BF16)
                        if not zero_state:
                            lhs = jnp.concatenate([lhs, cc * jnp.exp2(col).astype(BF16)], axis=1)
                        halves.append(jnp.dot(lhs, rhs, preferred_element_type=F32))
                    y_scr[rows, lanes] += jnp.where(first_half, halves[0], halves[1])
                w_exp = jnp.dot(w3_scr[rows, :], expand[d], preferred_element_type=F32)
                xd = (x_ref[rows, :].astype(F32) * w_exp).astype(BF16)
                states = jnp.dot(bt_scr[rows, :], xd, preferred_element_type=F32)
                if not zero_state:
                    states = h_in * cd_scr[d, pl.ds(pl.multiple_of(k * 8, 8), 1), :] + states
                h_scr[2 * sl + d] = states
        return carry

    if nc <= 2:
        for i in range(nc):
            scan(i, 0, zero_state=(i == 0 and not has_h0))
    else:
        lax.fori_loop(0, nc, scan, 0, unroll=2)

    def gate(k, carry):
        rows = pl.ds(pl.multiple_of(k * q, q), q)
        t = y_scr[rows, :] * _silu(z_ref[rows, :].astype(F32))
        yg_ref[rows, :] = (t * gw_ref[...]).astype(BF16)
        t2 = t * t
        ssq_ref[rows, :] = sum(t2[:, j * LANES:(j + 1) * LANES] for j in range(width // LANES))
        return carry

    lax.fori_loop(0, nct, gate, 0)

    if want_state:
        for sl in range(nb):
            for d in range(2):
                st_ref[sl, d] = jnp.transpose(h_scr[2 * sl + d])


def _ssd(xbc, proj, dt_raw, dt_bias, a_log, d_skip_exp, gnorm_w, h0, row0, n_seq, seq, n_heads, want_state, nb):
    d_inner = n_heads * HEAD_DIM
    hpg = n_heads // N_GROUPS
    width = hpg * HEAD_DIM
    nc = seq // CHUNK
    rows = nb * seq
    r0 = row0 // rows
    b0 = d_inner // D_STATE
    has_h0 = h0 is not None
    in_specs = [
        pl.BlockSpec((rows, width), lambda s, g: (s, g)),
        pl.BlockSpec((rows, D_STATE), lambda s, g: (s, b0 + g)),
        pl.BlockSpec((rows, D_STATE), lambda s, g: (s, b0 + N_GROUPS + g)),
        pl.BlockSpec((rows, 2 * n_heads), lambda s, g: (s + r0, 0)),
        pl.BlockSpec((2, n_heads), lambda s, g: (0, 0)),
        pl.BlockSpec((2, n_heads), lambda s, g: (0, 0)),
        pl.BlockSpec((1, width), lambda s, g: (0, g)),
        pl.BlockSpec((rows, width), lambda s, g: (s + r0, g)),
        pl.BlockSpec((1, width), lambda s, g: (0, g)),
    ]
    args = [xbc, xbc, xbc, dt_raw, dt_bias, a_log, d_skip_exp, proj, gnorm_w.reshape(1, d_inner)]
    if has_h0:
        in_specs.append(pl.BlockSpec((nb, 2, width, D_STATE), lambda s, g: (s, 0, g, 0)))
        args.append(h0)
    out_specs = [pl.BlockSpec((rows, width), lambda s, g: (s, g)), pl.BlockSpec((rows, LANES), lambda s, g: (s, g))]
    out_shape = [jax.ShapeDtypeStruct((n_seq * seq, d_inner), BF16), jax.ShapeDtypeStruct((n_seq * seq, N_GROUPS * LANES), F32)]
    if want_state:
        out_specs.append(pl.BlockSpec((nb, 2, width, D_STATE), lambda s, g: (s, 0, g, 0)))
        out_shape.append(jax.ShapeDtypeStruct((n_seq, 2, d_inner, D_STATE), F32))
    nct = nb * nc
    scratch = [
        pltpu.VMEM((2 * nb, D_STATE, width), F32),
        pltpu.VMEM((rows, width), F32),
        pltpu.VMEM((rows, LANES), F32),
        pltpu.VMEM((rows, LANES), BF16),
        pltpu.VMEM((nct * 2 * hpg, CHUNK), F32),
        pltpu.VMEM((rows, CHUNK), BF16),
        pltpu.VMEM((rows, CHUNK), BF16),
        pltpu.VMEM((2, 8 * nct, width), F32),
    ]
    need = (2 * (2 * rows * width * 2 + 2 * rows * D_STATE * 2 + rows * 2 * n_heads * 4 + rows * width * 2 + rows * LANES * 4)
            + ((4 if has_h0 else 0) + (4 if want_state else 0) + 2) * nb * width * D_STATE * 4
            + rows * width * 4 + rows * LANES * 12 + 16 * nct * width * 4 + 24 * CHUNK * width * 4)
    return pl.pallas_call(
        functools.partial(_ssd_kernel, nb=nb, nc=nc, hpg=hpg, has_h0=has_h0, want_state=want_state),
        grid=(n_seq // nb, N_GROUPS),
        in_specs=in_specs,
        out_specs=out_specs,
        out_shape=out_shape,
        scratch_shapes=scratch,
        compiler_params=_cparams(("arbitrary", "arbitrary"), need + 6 * 2**20),
        name="ssd_scan",
    )(*args)


def _final_norm_kernel(x_ref, w_ref, o_ref):
    x = x_ref[...]
    o_ref[...] = x * lax.rsqrt(jnp.mean(x * x, axis=-1, keepdims=True) + EPS) * w_ref[...]


def _final_norm(h, w):
    rows, d = h.shape
    tm = _TILES["norm_rows"]
    need = 2 * (2 * tm * d * 4) + 4 * tm * d * 4
    return pl.pallas_call(
        _final_norm_kernel,
        grid=(rows // tm,),
        in_specs=[pl.BlockSpec((tm, d), lambda i: (i, 0)), pl.BlockSpec((1, d), lambda i: (0, 0))],
        out_specs=pl.BlockSpec((tm, d), lambda i: (i, 0)),
        out_shape=jax.ShapeDtypeStruct((rows, d), F32),
        compiler_params=_cparams(("arbitrary",), need),
        name="final_norm",
    )(h, w.reshape(1, d))


def kernel(x_prompt, x_sample, state_ssd, c, c_ctx, ada_w, ada_b, norm_w, pool_in_w, pool_grp_w, pool_grp_b, pool_scale, pool_out_w, ssd_in_w, ssd_conv_w, ssd_conv_b, ssd_dt_bias, ssd_A_log, ssd_D, ssd_norm_w, ssd_out_w, final_norm_w):
    batch, seq, d = x_prompt.shape
    dec_batch, dec_seq, _ = x_sample.shape
    n_ctx, n_lat = batch * seq, dec_batch * dec_seq
    depth = ada_w.shape[0]
    n_heads = ssd_A_log.shape[-1]
    d_inner = n_heads * HEAD_DIM
    e_pool = pool_out_w.shape[1]

    c_rows = jnp.concatenate([c_ctx[None, :], c, jnp.zeros((8 - 1 - dec_batch, d), F32)], axis=0)
    mod = _ada_mod(c_rows, ada_w, ada_b).reshape(depth, 8, 3, d)
    nseg = 1 + dec_batch
    part = lambda layer, which: mod[layer, :nseg, which][:, None, :]

    h_ctx = x_prompt.reshape(n_ctx, d)
    h_lat = x_sample.reshape(n_lat, d)
    new_states = []
    carried = {}
    for layer in range(depth):
        j = layer // 2
        shift, scale, gate = (part(layer, k) for k in range(3))
        u = _prologue(h_ctx, h_lat, norm_w[layer].reshape(1, d), shift, scale, dec_seq)
        out_rider = None

        if layer % 2 == 0:
            nxt = layer + 1 < depth
            w_in = pool_in_w[j]
            w_fold, w_z = _matmul_grouped(w_in, pool_grp_w[j], BF16, rider=(w_in, 1, e_pool))
            y, w_out = _matmul(u, w_fold, BF16, rider=(pool_out_w[j], 0, d))
            z, carried["ssd_out"] = _matmul(u, w_z, BF16, rider=(ssd_out_w[j], 0, d) if nxt else None)
            v_ctx = _poolmix(y, z, _pool_membership(seq, None), pool_grp_b[j], pool_scale[j],
                             0, n_ctx, seq, _TILES["pool_ctx"], None)
            v_lat = _poolmix(y, z, _pool_membership(dec_seq, GRID_W), pool_grp_b[j], pool_scale[j],
                             n_ctx, n_lat, dec_seq, _TILES["pool_lat"], GRID_W)
            out_rider = (ssd_in_w[j], 0, ssd_in_w.shape[-1]) if nxt else None
            ssq_ctx = ssq_lat = None
        else:
            w_in = carried.pop("ssd_in")
            n_main = 2 * d_inner + 2 * N_GROUPS * D_STATE
            proj, _ = _matmul(u, w_in, BF16, n=n_main)
            dt_raw, _ = _matmul(u, w_in, F32, b_col0=n_main, n=2 * n_heads, bn=2 * n_heads)
            n_conv = n_main - d_inner
            d_exp = jnp.repeat(ssd_D[j], HEAD_DIM).reshape(1, d_inner)
            conv = lambda row0, n_rows, sq, rows: _conv_silu(proj, d_inner, n_conv, ssd_conv_w[j], ssd_conv_b[j], row0, n_rows, sq, rows)
            ssd = functools.partial(_ssd, proj=proj, dt_raw=dt_raw, dt_bias=ssd_dt_bias[j], a_log=ssd_A_log[j],
                                    d_skip_exp=d_exp, gnorm_w=ssd_norm_w[j], n_heads=n_heads)
            conv_rows = max(seq, min(_TILES["conv_rows"], n_ctx))
            v_ctx, ssq_ctx, st = ssd(conv(0, n_ctx, seq, conv_rows), h0=None, row0=0, n_seq=batch, seq=seq, want_state=True,
                                     nb=_TILES["ssd_seqs_ctx"])
            h0 = state_ssd[:, j].reshape(dec_batch, 2, d_inner, D_STATE)
            v_lat, ssq_lat = ssd(conv(n_ctx, n_lat, dec_seq, dec_seq), h0=h0, row0=n_ctx, n_seq=dec_batch, seq=dec_seq,
                                 want_state=False, nb=_TILES["ssd_seqs_lat"])
            new_states.append(st.reshape(batch, 2, n_heads, HEAD_DIM, D_STATE))
            w_out = carried.pop("ssd_out")

        h_ctx, carried["ssd_in"] = _matmul_residual(v_ctx, w_out, h_ctx, gate, 0, n_ctx, ssq_ctx, rider=out_rider)
        h_lat, _ = _matmul_residual(v_lat, w_out, h_lat, gate, 1, dec_seq, ssq_lat)

    y_prompt = _final_norm(h_ctx, final_norm_w).reshape(batch, seq, d)
    y_sample = _final_norm(h_lat, final_norm_w).reshape(dec_batch, dec_seq, d)
    return (y_prompt, y_sample, jnp.stack(new_states, axis=1))
```

```python
import functools

import jax
import jax.numpy as jnp
from jax import lax
from jax.experimental import pallas as pl
from jax.experimental.pallas import tpu as pltpu

F32 = jnp.float32
BF16 = jnp.bfloat16

EPS = 1e-6
POOL_WINDOWS = (2, 4, 8, 16)
GRID_W = 64
HEAD_DIM = 64
D_STATE = 128
N_GROUPS = 8
CONV_W = 7
CHUNK = 128
LANES = 128

V7X_VMEM_BYTES = 64 * 2**20
_VMEM_HEADROOM = 6 * 2**20

_TILES = dict(
    ada_bn=512,
    norm_rows=256,
    proj=(1024, 1024),
    fold=(512, 2048),
    out=(1024, 256, 8192),
    pool_ctx=(8, 1024),
    pool_lat=(1, 512),
    conv_cols=512,
    conv_rows=2048,
    ssd_seqs_ctx=4,
    ssd_seqs_lat=1,
)


def _cparams(semantics, vmem_need):
    limit = min(max(int(vmem_need), 16 * 2**20), V7X_VMEM_BYTES - _VMEM_HEADROOM)
    return pltpu.CompilerParams(dimension_semantics=semantics, vmem_limit_bytes=limit)


def _silu(t):
    return t * jax.nn.sigmoid(t)


def _seg(tile, rows_per_tile, n_ctx, dec_seq):
    start = tile * rows_per_tile
    return jnp.where(start < n_ctx, 0, 1 + jnp.maximum(start - n_ctx, 0) // dec_seq)


def _ada_kernel(c_ref, w_ref, b_ref, o_ref):
    s = _silu(c_ref[...]).astype(BF16)
    o_ref[0] = jnp.dot(s, w_ref[0].astype(BF16), preferred_element_type=F32) + b_ref[0]


def _ada_mod(c_rows, ada_w, ada_b):
    depth, d, n3 = ada_w.shape
    rows = c_rows.shape[0]
    bn = min(_TILES["ada_bn"], n3)
    need = 2 * (d * bn * 4) + d * bn * 2 + 4 * rows * d * 4
    return pl.pallas_call(
        _ada_kernel,
        grid=(depth, n3 // bn),
        in_specs=[
            pl.BlockSpec((rows, d), lambda l, j: (0, 0)),
            pl.BlockSpec((1, d, bn), lambda l, j: (l, 0, j)),
            pl.BlockSpec((1, 1, bn), lambda l, j: (l, 0, j)),
        ],
        out_specs=pl.BlockSpec((1, rows, bn), lambda l, j: (l, 0, j)),
        out_shape=jax.ShapeDtypeStruct((depth, rows, n3), F32),
        compiler_params=_cparams(("arbitrary", "arbitrary"), need + 8 * 2**20),
        name="ada_mod",
    )(c_rows, ada_w, ada_b.reshape(depth, 1, n3))


def _norm_mod(x, w, shift, scale):
    xn = x * lax.rsqrt(jnp.mean(x * x, axis=-1, keepdims=True) + EPS) * w
    return xn * (1.0 + scale) + shift


def _prologue_kernel(xc_ref, xl_ref, w_ref, sh_ref, sc_ref, u_ref, *, ctx_tiles):
    x = jnp.where(pl.program_id(0) < ctx_tiles, xc_ref[...], xl_ref[...])
    u_ref[...] = _norm_mod(x, w_ref[...], sh_ref[0], sc_ref[0]).astype(BF16)


def _prologue(h_ctx, h_lat, w, shift, scale, dec_seq):
    n_ctx, d = h_ctx.shape
    t = n_ctx + h_lat.shape[0]
    tm = _TILES["norm_rows"]
    ctx_tiles = n_ctx // tm
    seg = lambda i: (_seg(i, tm, n_ctx, dec_seq), 0, 0)
    need = 2 * (2 * tm * d * 4 + tm * d * 2) + 6 * tm * d * 4
    return pl.pallas_call(
        functools.partial(_prologue_kernel, ctx_tiles=ctx_tiles),
        grid=(t // tm,),
        in_specs=[
            pl.BlockSpec((tm, d), lambda i: (jnp.minimum(i, ctx_tiles - 1), 0)),
            pl.BlockSpec((tm, d), lambda i: (jnp.maximum(i - ctx_tiles, 0), 0)),
            pl.BlockSpec((1, d), lambda i: (0, 0)),
            pl.BlockSpec((1, 1, d), seg),
            pl.BlockSpec((1, 1, d), seg),
        ],
        out_specs=pl.BlockSpec((tm, d), lambda i: (i, 0)),
        out_shape=jax.ShapeDtypeStruct((t, d), BF16),
        compiler_params=_cparams(("arbitrary",), need),
        name="prologue",
    )(h_ctx, h_lat, w, shift, scale)


def _rider_specs(rider, n_steps, step_of):
    src, col_block, n_cols = rider
    r = src.shape[0]
    rb = 16
    while r % rb or r // rb > n_steps:
        rb += 16
    last = r // rb - 1
    return (pl.BlockSpec((rb, n_cols), lambda *ids: (jnp.minimum(step_of(*ids), last), col_block)),
            pl.BlockSpec((rb, n_cols), lambda *ids: (jnp.minimum(step_of(*ids), last), 0)),
            jax.ShapeDtypeStruct((r, n_cols), BF16), 2 * rb * n_cols * (4 + 2))


def _mm_kernel(*refs, has_rider, b_index):
    a_ref, b_ref = refs[:2]
    o_ref = refs[3 if has_rider else 2]
    b = b_ref[...] if b_index is None else b_ref[b_index]
    o_ref[...] = jnp.dot(a_ref[...].astype(BF16), b.astype(BF16), preferred_element_type=F32).astype(o_ref.dtype)
    if has_rider:
        refs[4][...] = refs[2][...].astype(BF16)


def _matmul(a, b, out_dtype, b_col0=0, n=None, bn=None, rider=None):
    m, k = a.shape
    n = b.shape[1] if n is None else n
    bm = min(_TILES["proj"][0], m)
    bn = min(_TILES["proj"][1] if bn is None else bn, n)
    j0 = b_col0 // bn
    nj = n // bn
    osz = jnp.dtype(out_dtype).itemsize
    need = 2 * (bm * k * 2 + k * bn * 2 + bm * bn * osz) + bm * bn * 4
    in_specs = [pl.BlockSpec((bm, k), lambda i, j: (i, 0)), pl.BlockSpec((k, bn), lambda i, j: (0, j + j0))]
    out_specs = [pl.BlockSpec((bm, bn), lambda i, j: (i, j))]
    out_shape = [jax.ShapeDtypeStruct((m, n), out_dtype)]
    args = [a, b]
    if rider is not None:
        r_in, r_out, r_shape, r_bytes = _rider_specs(rider, (m // bm) * nj, lambda i, j: i * nj + j)
        in_specs.append(r_in), out_specs.append(r_out), out_shape.append(r_shape), args.append(rider[0])
        need += r_bytes
    res = pl.pallas_call(
        functools.partial(_mm_kernel, has_rider=rider is not None, b_index=None),
        grid=(m // bm, nj),
        in_specs=in_specs,
        out_specs=out_specs,
        out_shape=out_shape,
        compiler_params=_cparams(("arbitrary", "arbitrary"), need + 4 * 2**20),
        name="matmul",
    )(*args)
    return res[0], (res[1] if rider is not None else None)


def _matmul_grouped(a, b, out_dtype, rider=None):
    m = a.shape[0]
    groups, k, n = b.shape
    bm, bn = min(_TILES["fold"][0], m), min(_TILES["fold"][1], n)
    ni, nj = m // bm, n // bn
    osz = jnp.dtype(out_dtype).itemsize
    need = (2 * (bm * k * a.dtype.itemsize + k * bn * b.dtype.itemsize + bm * bn * osz)
            + bm * bn * 4 + (bm * k + k * bn) * 2)
    in_specs = [pl.BlockSpec((bm, k), lambda g, j, i: (i, g)), pl.BlockSpec((1, k, bn), lambda g, j, i: (g, 0, j))]
    out_specs = [pl.BlockSpec((bm, bn), lambda g, j, i: (i, g * nj + j))]
    out_shape = [jax.ShapeDtypeStruct((m, groups * n), out_dtype)]
    args = [a, b]
    if rider is not None:
        r_in, r_out, r_shape, r_bytes = _rider_specs(rider, groups * ni * nj, lambda g, j, i: (g * nj + j) * ni + i)
        in_specs.append(r_in), out_specs.append(r_out), out_shape.append(r_shape), args.append(rider[0])
        need += r_bytes
    res = pl.pallas_call(
        functools.partial(_mm_kernel, has_rider=rider is not None, b_index=0),
        grid=(groups, nj, ni),
        in_specs=in_specs,
        out_specs=out_specs,
        out_shape=out_shape,
        compiler_params=_cparams(("arbitrary", "arbitrary", "arbitrary"), need + 4 * 2**20),
        name="matmul_grouped",
    )(*args)
    return res[0], (res[1] if rider is not None else None)


def _mm_res_kernel(*refs, nk, norm_dim, has_rider):
    it = iter(refs)
    a_ref, b_ref, h_ref, g_ref = (next(it) for _ in range(4))
    ssq_ref = next(it) if norm_dim else None
    rsrc_ref = next(it) if has_rider else None
    o_ref = next(it)
    if has_rider:
        next(it)[...] = rsrc_ref[...].astype(BF16)

    def finish(acc):
        if norm_dim:
            ms = jnp.sum(ssq_ref[...], axis=1, keepdims=True) * (1.0 / norm_dim)
            acc = acc * lax.rsqrt(ms + EPS)
        o_ref[...] = h_ref[...] + g_ref[0] * acc

    if nk == 1:
        finish(jnp.dot(a_ref[...], b_ref[...], preferred_element_type=F32))
        return
    acc_ref = refs[-1]
    k = pl.program_id(2)

    @pl.when(k == 0)
    def _():
        acc_ref[...] = jnp.zeros_like(acc_ref)

    acc_ref[...] += jnp.dot(a_ref[...], b_ref[...], preferred_element_type=F32)

    @pl.when(k == nk - 1)
    def _():
        finish(acc_ref[...])


def _matmul_residual(a, b, h, gate, seg0, seg_rows, ssq=None, rider=None):
    m, kdim = a.shape
    n = b.shape[1]
    bm, bn, bk = (min(t, s) for t, s in zip(_TILES["out"], (m, n, kdim)))
    a_mode = {}
    if rider is None and bk == kdim and n % (2 * bn) == 0:
        bn, a_mode = 2 * bn, dict(pipeline_mode=pl.Buffered(1))
    nj, nk = n // bn, kdim // bk
    need = (1 if a_mode else 2) * bm * bk * 2 + 2 * (bk * bn * 2 + 2 * bm * bn * 4) + 2 * bm * bn * 4
    in_specs = [
        pl.BlockSpec((bm, bk), lambda i, j, k: (i, k), **a_mode),
        pl.BlockSpec((bk, bn), lambda i, j, k: (k, j)),
        pl.BlockSpec((bm, bn), lambda i, j, k: (i, j)),
        pl.BlockSpec((1, 1, bn), lambda i, j, k: (seg0 + (i * bm) // seg_rows, 0, j)),
    ]
    args = [a, b, h, gate]
    if ssq is not None:
        in_specs.append(pl.BlockSpec((bm, ssq.shape[1]), lambda i, j, k: (i, 0)))
        args.append(ssq)
        need += 2 * bm * ssq.shape[1] * 4
    out_specs = [pl.BlockSpec((bm, bn), lambda i, j, k: (i, j))]
    out_shape = [jax.ShapeDtypeStruct((m, n), F32)]
    if rider is not None:
        r_in, r_out, r_shape, r_bytes = _rider_specs(rider, (m // bm) * nj * nk, lambda i, j, k: (i * nj + j) * nk + k)
        in_specs.append(r_in), out_specs.append(r_out), out_shape.append(r_shape), args.append(rider[0])
        need += r_bytes
    res = pl.pallas_call(
        functools.partial(_mm_res_kernel, nk=nk, norm_dim=kdim if ssq is not None else 0, has_rider=rider is not None),
        grid=(m // bm, nj, nk),
        in_specs=in_specs,
        out_specs=out_specs,
        out_shape=out_shape,
        scratch_shapes=[pltpu.VMEM((bm, bn), F32)] if nk > 1 else [],
        compiler_params=_cparams(("arbitrary", "arbitrary", "arbitrary"), need + 4 * 2**20),
        name="matmul_residual",
    )(*args)
    return res[0], (res[1] if rider is not None else None)


def _window_members(pos_t, pos_s, w):
    lo = pos_t - w // 2
    return (pos_s >= lo) & (pos_s < lo + w)


def _pool_membership(seq, grid_w):
    t = jnp.arange(seq, dtype=jnp.int32)[:, None]
    s = jnp.arange(seq, dtype=jnp.int32)[None, :]
    mats = []
    for w in POOL_WINDOWS:
        if grid_w is None:
            m = _window_members(t, s, w)
        else:
            m = _window_members(t // grid_w, s // grid_w, w) & _window_members(t % grid_w, s % grid_w, w)
        mats.append(m)
    return jnp.stack(mats).astype(BF16)


def _poolmix_kernel(y_ref, z_ref, p_ref, b_ref, sc_ref, o_ref, *, seq, nseq, rows_step, reaches, cols_per_group):
    bias = b_ref[...]
    scale = sc_ref[...]

    def mix(reach):
        for s in range(nseq):
            for r0 in range(0, seq, rows_step):
                k0, k1 = max(0, r0 - reach), min(seq, r0 + rows_step + reach)
                member = p_ref[0, pl.ds(r0, rows_step), pl.ds(k0, k1 - k0)]
                inv_cnt = 1.0 / jnp.sum(member.astype(F32), axis=1, keepdims=True)
                rows = pl.ds(s * seq + r0, rows_step)
                tot = jnp.dot(member, y_ref[pl.ds(s * seq + k0, k1 - k0), :], preferred_element_type=F32)
                mixed = tot * inv_cnt - y_ref[rows, :].astype(F32) + bias
                z = z_ref[rows, :].astype(F32)
                o_ref[rows, :] = (mixed * scale * _silu(z)).astype(BF16)

    if seq <= rows_step:
        mix(seq)
    else:
        group = pl.program_id(0) // cols_per_group
        for gi, reach in enumerate(reaches):
            pl.when(group == gi)(functools.partial(mix, reach))


def _poolmix(y, z, member, grp_b, scale, row0, n_rows, seq, tile, grid_w):
    e = y.shape[1]
    gw = e // len(POOL_WINDOWS)
    nseq, cb = min(tile[0], n_rows // seq), min(tile[1], gw)
    rows = seq * nseq
    r0 = row0 // rows
    rows_step = min(seq, 512)
    reaches = tuple(-(-(w // 2) * (grid_w or 1) // LANES) * LANES for w in POOL_WINDOWS)
    need = 2 * (3 * rows * cb * 2 + seq * seq * 2) + 6 * rows_step * cb * 4 + rows_step * seq * 4
    return pl.pallas_call(
        functools.partial(_poolmix_kernel, seq=seq, nseq=nseq, rows_step=rows_step, reaches=reaches, cols_per_group=gw // cb),
        grid=(e // cb, n_rows // rows),
        in_specs=[
            pl.BlockSpec((rows, cb), lambda c, i: (i + r0, c)),
            pl.BlockSpec((rows, cb), lambda c, i: (i + r0, c)),
            pl.BlockSpec((1, seq, seq), lambda c, i: (c * cb // gw, 0, 0)),
            pl.BlockSpec((1, cb), lambda c, i: (0, c)),
            pl.BlockSpec((1, cb), lambda c, i: (0, c)),
        ],
        out_specs=pl.BlockSpec((rows, cb), lambda c, i: (i, c)),
        out_shape=jax.ShapeDtypeStruct((n_rows, e), BF16),
        compiler_params=_cparams(("arbitrary", "arbitrary"), need + 8 * 2**20),
        name="poolmix",
    )(y, z, member, grp_b.reshape(1, e), scale.reshape(1, e))


_CONV_PAD = 16
_CONV_ROWS = 128


def _conv_kernel(x_ref, w_ref, b_ref, o_ref, xpad_ref, *, seq):
    rows, cb = x_ref.shape
    half = CONV_W // 2
    pitch = seq + _CONV_PAD
    win = _CONV_ROWS + 2 * _CONV_PAD
    taps = [k for k in range(CONV_W) if k != half]
    out_row = lax.broadcasted_iota(jnp.int32, (len(taps) * _CONV_ROWS, win), 0)
    src_row = lax.broadcasted_iota(jnp.int32, (len(taps) * _CONV_ROWS, win), 1)
    offset = sum(jnp.where(out_row // _CONV_ROWS == i, k - half, 0) for i, k in enumerate(taps))
    shift_all = (src_row == out_row % _CONV_ROWS + _CONV_PAD + offset).astype(BF16)

    for s in range(rows // seq + 1):
        xpad_ref[pl.ds(s * pitch, _CONV_PAD), :] = jnp.zeros((_CONV_PAD, cb), BF16)
    for s in range(rows // seq):
        xpad_ref[pl.ds(s * pitch + _CONV_PAD, seq), :] = x_ref[pl.ds(s * seq, seq), :]
    bias = b_ref[...]
    for s in range(rows // seq):
        for j in range(seq // _CONV_ROWS):
            base = s * pitch + _CONV_PAD + j * _CONV_ROWS
            shifted = jnp.dot(shift_all, xpad_ref[pl.ds(base - _CONV_PAD, win), :], preferred_element_type=F32)
            out_rows = pl.ds(s * seq + j * _CONV_ROWS, _CONV_ROWS)
            acc = bias + w_ref[pl.ds(half, 1), :] * x_ref[out_rows, :].astype(F32)
            for i, k in enumerate(taps):
                acc = acc + w_ref[pl.ds(k, 1), :] * shifted[i * _CONV_ROWS:(i + 1) * _CONV_ROWS]
            o_ref[out_rows, :] = _silu(acc).astype(BF16)


def _conv_silu(proj, col0, n_cols, conv_w, conv_b, row0, n_rows, seq, rows):
    assert rows % seq == 0 and seq % _CONV_ROWS == 0 and CONV_W // 2 <= _CONV_PAD
    cb = _TILES["conv_cols"]
    r0 = row0 // rows
    c0 = col0 // cb
    padded = (rows // seq) * (seq + _CONV_PAD) + _CONV_PAD
    need = 2 * (2 * rows * cb * 2) + padded * cb * 2 + 16 * _CONV_ROWS * cb * 4
    return pl.pallas_call(
        functools.partial(_conv_kernel, seq=seq),
        grid=(n_rows // rows, n_cols // cb),
        in_specs=[
            pl.BlockSpec((rows, cb), lambda i, c: (i + r0, c + c0)),
            pl.BlockSpec((CONV_W, cb), lambda i, c: (0, c)),
            pl.BlockSpec((1, cb), lambda i, c: (0, c)),
        ],
        out_specs=pl.BlockSpec((rows, cb), lambda i, c: (i, c)),
        out_shape=jax.ShapeDtypeStruct((n_rows, n_cols), BF16),
        scratch_shapes=[pltpu.VMEM((padded, cb), BF16)],
        compiler_params=_cparams(("arbitrary", "arbitrary"), need + 4 * 2**20),
        name="conv_silu",
    )(proj, conv_w, conv_b.reshape(1, n_cols))


_LOG2E = 1.4426950408889634


def _split3(v):
    hi = v.astype(BF16)
    r1 = v - hi.astype(F32)
    mid = r1.astype(BF16)
    lo = (r1 - mid.astype(F32)).astype(BF16)
    return hi, mid, lo


def _pack3(v, stride):
    hi, mid, lo = _split3(v)
    packed = hi.astype(F32) + pltpu.roll(mid.astype(F32), stride, axis=1) + pltpu.roll(lo.astype(F32), 2 * stride, axis=1)
    return packed.astype(BF16)


def _ssd_kernel(*refs, nb, nc, hpg, has_h0, want_state):
    it = iter(refs)
    x_ref, b_ref, c_ref, dt_ref, dtb_ref, alog_ref, dskip_ref, z_ref, gw_ref = (next(it) for _ in range(9))
    h0_ref = next(it) if has_h0 else None
    yg_ref, ssq_ref = next(it), next(it)
    st_ref = next(it) if want_state else None
    h_scr, y_scr, cum_scr, w3_scr, rowp_scr, bt_scr, s_scr, cd_scr = (next(it) for _ in range(8))

    q = CHUNK
    nct = nb * nc
    width = hpg * HEAD_DIM
    stride3 = 2 * hpg
    assert q == D_STATE == LANES and hpg % 2 == 0 and 2 * HEAD_DIM == LANES and 3 * stride3 <= LANES
    g = pl.program_id(1)
    shift0 = (LANES - hpg * g) % LANES
    shifts = (shift0, (shift0 + hpg) % LANES)

    lane = lax.broadcasted_iota(jnp.int32, (q, LANES), 1)
    lane_all = lax.broadcasted_iota(jnp.int32, (nct * q, LANES), 1)
    lane1 = lax.broadcasted_iota(jnp.int32, (1, LANES), 1)
    row_i = lax.broadcasted_iota(jnp.int32, (q, q), 0)
    col_i = lax.broadcasted_iota(jnp.int32, (q, q), 1)
    masks = (row_i >= col_i, row_i <= col_i)
    tri = tuple(m.astype(BF16) for m in masks)
    ones = jnp.ones((q, q), BF16)
    e_row = lax.broadcasted_iota(jnp.int32, (LANES, width), 0)
    e_head = lax.broadcasted_iota(jnp.int32, (LANES, width), 1) // HEAD_DIM
    expand = tuple(((e_row % stride3 == d * hpg + e_head) & (e_row < 3 * stride3)).astype(BF16) for d in range(2))
    first_half = lane < HEAD_DIM

    def pick(v0, v1, ln):
        return jnp.where(ln < hpg, v0, jnp.where(ln < 2 * hpg, v1, 0.0))

    def unpack3(p):
        s = p + pltpu.roll(p, LANES - stride3, axis=1) + pltpu.roll(p, LANES - 2 * stride3, axis=1)
        return jnp.where(lane_all < stride3, s, 0.0)

    for sl in range(nb):
        for d in range(2):
            if has_h0:
                h_scr[2 * sl + d] = jnp.transpose(h0_ref[sl, d])
            elif nc > 2:
                h_scr[2 * sl + d] = jnp.zeros((D_STATE, width), F32)

    neg_a = pick(*(-jnp.exp(pltpu.roll(alog_ref[pl.ds(d, 1), :], shifts[d], axis=1)) for d in range(2)), lane1)
    dt_bias = pick(*(pltpu.roll(dtb_ref[pl.ds(d, 1), :], shifts[d], axis=1) for d in range(2)), lane1)
    dt_raw = pick(*(pltpu.roll(dt_ref[:, pl.ds(d * LANES, LANES)], shifts[d], axis=1) for d in range(2)), lane_all)
    dt = jnp.where(lane_all < 2 * hpg, jax.nn.softplus(dt_raw + dt_bias), 0.0)
    a3 = _pack3(dt * neg_a, stride3)
    chunks = [a3[k * q:(k + 1) * q] for k in range(nct)]
    pfx, sfx, tot = (unpack3(jnp.concatenate([jnp.dot(m, ck, preferred_element_type=F32) for ck in chunks], axis=0))
                     for m in (tri[0], tri[1], ones))
    cum = jnp.where(lane_all < hpg, pfx, sfx)
    cum_scr[...] = cum * _LOG2E
    w3_scr[...] = _pack3(dt * jnp.exp(tot - cum), stride3)
    rowp = (cum - jnp.log(dt)) * _LOG2E
    decay8 = jnp.concatenate([jnp.exp(tot[k * q:k * q + 8]) for k in range(nct)], axis=0)
    lane8 = lax.broadcasted_iota(jnp.int32, (8 * nct, LANES), 1)
    cd3 = _pack3(jnp.where(lane8 < stride3, decay8, 0.0), stride3)
    for d in range(2):
        cd_scr[d] = jnp.dot(cd3, expand[d], preferred_element_type=F32)
    for k in range(nct):
        rows = pl.ds(k * q, q)
        rowp_scr[pl.ds(k * stride3, stride3), :] = jnp.transpose(rowp[k * q:(k + 1) * q])[:stride3, :]
        bc = b_ref[rows, :]
        bt_scr[rows, :] = jnp.transpose(bc.astype(F32)).astype(BF16)
        s_scr[rows, :] = lax.dot_general(c_ref[rows, :], bc, (((1,), (1,)), ((), ())), preferred_element_type=F32).astype(BF16)
    y_scr[...] = x_ref[...].astype(F32) * dskip_ref[...]

    def scan(i, carry, zero_state=False):
        for sl in range(nb):
            for d in range(2):
                k = sl * nc + (i if d == 0 else nc - 1 - i)
                rows = pl.ds(pl.multiple_of(k * q, q), q)
                cum_c = cum_scr[rows, :]
                scores = s_scr[rows, :]
                cc = c_ref[rows, :]
                if not zero_state:
                    h_in = h_scr[2 * sl + d]
                    h_bf = h_in.astype(BF16)
                for p in range(hpg // 2):
                    lanes = pl.ds(p * LANES, LANES)
                    rhs = x_ref[rows, lanes]
                    if not zero_state:
                        rhs = jnp.concatenate([rhs, h_bf[:, p * LANES:(p + 1) * LANES]], axis=0)
                    halves = []
                    for r in (2 * p, 2 * p + 1):
                        col = jnp.broadcast_to(cum_c[:, d * hpg + r:d * hpg + r + 1], (q, q))
                        row = rowp_scr[pl.ds(k * stride3 + d * hpg + r, 1), :]
                        lhs = scores * jnp.exp2(jnp.where(masks[d], col - row, -jnp.inf)).astype(BF16)
                        if not zero_state:
                            lhs = jnp.concatenate([lhs, cc * jnp.exp2(col).astype(BF16)], axis=1)
                        halves.append(jnp.dot(lhs, rhs, preferred_element_type=F32))
                    y_scr[rows, lanes] += jnp.where(first_half, halves[0], halves[1])
                w_exp = jnp.dot(w3_scr[rows, :], expand[d], preferred_element_type=F32)
                xd = (x_ref[rows, :].astype(F32) * w_exp).astype(BF16)
                states = jnp.dot(bt_scr[rows, :], xd, preferred_element_type=F32)
                if not zero_state:
                    states = h_in * cd_scr[d, pl.ds(pl.multiple_of(k * 8, 8), 1), :] + states
                h_scr[2 * sl + d] = states
        return carry

    if nc <= 2:
        for i in range(nc):
            scan(i, 0, zero_state=(i == 0 and not has_h0))
    else:
        lax.fori_loop(0, nc, scan, 0, unroll=2)

    def gate(k, carry):
        rows = pl.ds(pl.multiple_of(k * q, q), q)
        t = y_scr[rows, :] * _silu(z_ref[rows, :].astype(F32))
        yg_ref[rows, :] = (t * gw_ref[...]).astype(BF16)
        t2 = t * t
        ssq_ref[rows, :] = sum(t2[:, j * LANES:(j + 1) * LANES] for j in range(width // LANES))
        return carry

    lax.fori_loop(0, nct, gate, 0)

    if want_state:
        for sl in range(nb):
            for d in range(2):
                st_ref[sl, d] = jnp.transpose(h_scr[2 * sl + d])


def _ssd(xbc, proj, dt_raw, dt_bias, a_log, d_skip_exp, gnorm_w, h0, row0, n_seq, seq, n_heads, want_state, nb):
    d_inner = n_heads * HEAD_DIM
    hpg = n_heads // N_GROUPS
    width = hpg * HEAD_DIM
    nc = seq // CHUNK
    rows = nb * seq
    r0 = row0 // rows
    b0 = d_inner // D_STATE
    has_h0 = h0 is not None
    in_specs = [
        pl.BlockSpec((rows, width), lambda s, g: (s, g)),
        pl.BlockSpec((rows, D_STATE), lambda s, g: (s, b0 + g)),
        pl.BlockSpec((rows, D_STATE), lambda s, g: (s, b0 + N_GROUPS + g)),
        pl.BlockSpec((rows, 2 * n_heads), lambda s, g: (s + r0, 0)),
        pl.BlockSpec((2, n_heads), lambda s, g: (0, 0)),
        pl.BlockSpec((2, n_heads), lambda s, g: (0, 0)),
        pl.BlockSpec((1, width), lambda s, g: (0, g)),
        pl.BlockSpec((rows, width), lambda s, g: (s + r0, g)),
        pl.BlockSpec((1, width), lambda s, g: (0, g)),
    ]
    args = [xbc, xbc, xbc, dt_raw, dt_bias, a_log, d_skip_exp, proj, gnorm_w.reshape(1, d_inner)]
    if has_h0:
        in_specs.append(pl.BlockSpec((nb, 2, width, D_STATE), lambda s, g: (s, 0, g, 0)))
        args.append(h0)
    out_specs = [pl.BlockSpec((rows, width), lambda s, g: (s, g)), pl.BlockSpec((rows, LANES), lambda s, g: (s, g))]
    out_shape = [jax.ShapeDtypeStruct((n_seq * seq, d_inner), BF16), jax.ShapeDtypeStruct((n_seq * seq, N_GROUPS * LANES), F32)]
    if want_state:
        out_specs.append(pl.BlockSpec((nb, 2, width, D_STATE), lambda s, g: (s, 0, g, 0)))
        out_shape.append(jax.ShapeDtypeStruct((n_seq, 2, d_inner, D_STATE), F32))
    nct = nb * nc
    scratch = [
        pltpu.VMEM((2 * nb, D_STATE, width), F32),
        pltpu.VMEM((rows, width), F32),
        pltpu.VMEM((rows, LANES), F32),
        pltpu.VMEM((rows, LANES), BF16),
        pltpu.VMEM((nct * 2 * hpg, CHUNK), F32),
        pltpu.VMEM((rows, CHUNK), BF16),
        pltpu.VMEM((rows, CHUNK), BF16),
        pltpu.VMEM((2, 8 * nct, width), F32),
    ]
    need = (2 * (2 * rows * width * 2 + 2 * rows * D_STATE * 2 + rows * 2 * n_heads * 4 + rows * width * 2 + rows * LANES * 4)
            + ((4 if has_h0 else 0) + (4 if want_state else 0) + 2) * nb * width * D_STATE * 4
            + rows * width * 4 + rows * LANES * 12 + 16 * nct * width * 4 + 24 * CHUNK * width * 4)
    return pl.pallas_call(
        functools.partial(_ssd_kernel, nb=nb, nc=nc, hpg=hpg, has_h0=has_h0, want_state=want_state),
        grid=(n_seq // nb, N_GROUPS),
        in_specs=in_specs,
        out_specs=out_specs,
        out_shape=out_shape,
        scratch_shapes=scratch,
        compiler_params=_cparams(("arbitrary", "arbitrary"), need + 6 * 2**20),
        name="ssd_scan",
    )(*args)


def _final_norm_kernel(x_ref, w_ref, o_ref):
    x = x_ref[...]
    o_ref[...] = x * lax.rsqrt(jnp.mean(x * x, axis=-1, keepdims=True) + EPS) * w_ref[...]


def _final_norm(h, w):
    rows, d = h.shape
    tm = _TILES["norm_rows"]
    need = 2 * (2 * tm * d * 4) + 4 * tm * d * 4
    return pl.pallas_call(
        _final_norm_kernel,
        grid=(rows // tm,),
        in_specs=[pl.BlockSpec((tm, d), lambda i: (i, 0)), pl.BlockSpec((1, d), lambda i: (0, 0))],
        out_specs=pl.BlockSpec((tm, d), lambda i: (i, 0)),
        out_shape=jax.ShapeDtypeStruct((rows, d), F32),
        compiler_params=_cparams(("arbitrary",), need),
        name="final_norm",
    )(h, w.reshape(1, d))


def kernel(x_prompt, x_sample, state_ssd, c, c_ctx, ada_w, ada_b, norm_w, pool_in_w, pool_grp_w, pool_grp_b, pool_scale, pool_out_w, ssd_in_w, ssd_conv_w, ssd_conv_b, ssd_dt_bias, ssd_A_log, ssd_D, ssd_norm_w, ssd_out_w, final_norm_w):
    batch, seq, d = x_prompt.shape
    dec_batch, dec_seq, _ = x_sample.shape
    n_ctx, n_lat = batch * seq, dec_batch * dec_seq
    depth = ada_w.shape[0]
    n_heads = ssd_A_log.shape[-1]
    d_inner = n_heads * HEAD_DIM
    e_pool = pool_out_w.shape[1]

    c_rows = jnp.concatenate([c_ctx[None, :], c, jnp.zeros((8 - 1 - dec_batch, d), F32)], axis=0)
    mod = _ada_mod(c_rows, ada_w, ada_b).reshape(depth, 8, 3, d)
    nseg = 1 + dec_batch
    part = lambda layer, which: mod[layer, :nseg, which][:, None, :]

    h_ctx = x_prompt.reshape(n_ctx, d)
    h_lat = x_sample.reshape(n_lat, d)
    new_states = []
    carried = {}
    for layer in range(depth):
        j = layer // 2
        shift, scale, gate = (part(layer, k) for k in range(3))
        u = _prologue(h_ctx, h_lat, norm_w[layer].reshape(1, d), shift, scale, dec_seq)
        out_rider = None

        if layer % 2 == 0:
            nxt = layer + 1 < depth
            w_in = pool_in_w[j]
            w_fold, w_z = _matmul_grouped(w_in, pool_grp_w[j], BF16, rider=(w_in, 1, e_pool))
            y, w_out = _matmul(u, w_fold, BF16, rider=(pool_out_w[j], 0, d))
            z, carried["ssd_out"] = _matmul(u, w_z, BF16, rider=(ssd_out_w[j], 0, d) if nxt else None)
            v_ctx = _poolmix(y, z, _pool_membership(seq, None), pool_grp_b[j], pool_scale[j],
                             0, n_ctx, seq, _TILES["pool_ctx"], None)
            v_lat = _poolmix(y, z, _pool_membership(dec_seq, GRID_W), pool_grp_b[j], pool_scale[j],
                             n_ctx, n_lat, dec_seq, _TILES["pool_lat"], GRID_W)
            out_rider = (ssd_in_w[j], 0, ssd_in_w.shape[-1]) if nxt else None
            ssq_ctx = ssq_lat = None
        else:
            w_in = carried.pop("ssd_in")
            n_main = 2 * d_inner + 2 * N_GROUPS * D_STATE
            proj, _ = _matmul(u, w_in, BF16, n=n_main)
            dt_raw, _ = _matmul(u, w_in, F32, b_col0=n_main, n=2 * n_heads, bn=2 * n_heads)
            n_conv = n_main - d_inner
            d_exp = jnp.repeat(ssd_D[j], HEAD_DIM).reshape(1, d_inner)
            conv = lambda row0, n_rows, sq, rows: _conv_silu(proj, d_inner, n_conv, ssd_conv_w[j], ssd_conv_b[j], row0, n_rows, sq, rows)
            ssd = functools.partial(_ssd, proj=proj, dt_raw=dt_raw, dt_bias=ssd_dt_bias[j], a_log=ssd_A_log[j],
                                    d_skip_exp=d_exp, gnorm_w=ssd_norm_w[j], n_heads=n_heads)
            conv_rows = max(seq, min(_TILES["conv_rows"], n_ctx))
            v_ctx, ssq_ctx, st = ssd(conv(0, n_ctx, seq, conv_rows), h0=None, row0=0, n_seq=batch, seq=seq, want_state=True,
                                     nb=_TILES["ssd_seqs_ctx"])
            h0 = state_ssd[:, j].reshape(dec_batch, 2, d_inner, D_STATE)
            v_lat, ssq_lat = ssd(conv(n_ctx, n_lat, dec_seq, dec_seq), h0=h0, row0=n_ctx, n_seq=dec_batch, seq=dec_seq,
                                 want_state=False, nb=_TILES["ssd_seqs_lat"])
            new_states.append(st.reshape(batch, 2, n_heads, HEAD_DIM, D_STATE))
            w_out = carried.pop("ssd_out")

        h_ctx, carried["ssd_in"] = _matmul_residual(v_ctx, w_out, h_ctx, gate, 0, n_ctx, ssq_ctx, rider=out_rider)
        h_lat, _ = _matmul_residual(v_lat, w_out, h_lat, gate, 1, dec_seq, ssq_lat)

    y_prompt = _final_norm(h_ctx, final_norm_w).reshape(batch, seq, d)
    y_sample = _final_norm(h_lat, final_norm_w).reshape(dec_batch, dec_seq, d)
    return (y_prompt, y_sample, jnp.stack(new_states, axis=1))
```
